```python
import jax, jax.numpy as jnp
from jax import lax
import numpy as np

D_MODEL = 2048
BATCH = 8
SEQ = 2048
DEPTH = 2

D_MIX = D_MODEL
D_CONV = D_MIX // 2
CONV_GROUPS = 8
CONV_WIDTH = 31
D_SSD = D_MIX - D_CONV
SSD_HEAD_DIM = 64
SSD_HEADS = D_SSD // SSD_HEAD_DIM
SSD_GROUPS = 2
SSD_STATE = 128
SSD_CONV_WIDTH = 4
SSD_CHUNK = 128
SSD_XBC = D_SSD + 2 * SSD_GROUPS * SSD_STATE
D_IN_TOTAL = 2 * D_CONV + D_SSD + SSD_XBC + SSD_HEADS
N_EXPERTS = 32
N_EXPERT_GROUPS = 8
EXPERTS_PER_GROUP = N_EXPERTS // N_EXPERT_GROUPS
TOP_K = 2
D_EXPERT = 512
MOE_BLOCK = 128
EPS = 1e-6

kernel_name = "hymba_conformer_ssd_grouped_moe_adaln"


def rms_norm(x, w):
    xf = x.astype(jnp.float32)
    y = xf * lax.rsqrt(jnp.mean(xf * xf, axis=-1, keepdims=True) + EPS)
    return (y * w.astype(jnp.float32)).astype(x.dtype)


def modulate(h, shift, scale):
    return h * (1 + scale[:, None, :]) + shift[:, None, :]


def causal_depthwise_conv(x, w, b):
    k, ch = w.shape
    y = lax.conv_general_dilated(
        x, w[:, None, :].astype(x.dtype), window_strides=(1,),
        padding=((k - 1, 0),), dimension_numbers=("NWC", "WIO", "NWC"),
        feature_group_count=ch)
    return y + b.astype(x.dtype)


def conformer_conv_group(a, g, dw_w, dw_b, ln_w, ln_b):
    u = a * jax.nn.sigmoid(g)
    u = causal_depthwise_conv(u, dw_w, dw_b)
    bsz, seq, ch = u.shape
    uf = u.astype(jnp.float32).reshape(bsz, seq, CONV_GROUPS, ch // CONV_GROUPS)
    mu = jnp.mean(uf, axis=-1, keepdims=True)
    var = jnp.mean(jnp.square(uf - mu), axis=-1, keepdims=True)
    uf = ((uf - mu) * lax.rsqrt(var + EPS)).reshape(bsz, seq, ch)
    uf = uf * ln_w.astype(jnp.float32) + ln_b.astype(jnp.float32)
    return jax.nn.silu(uf).astype(a.dtype)


def segsum(a):
    cs = jnp.cumsum(a, axis=-1)
    diff = cs[..., :, None] - cs[..., None, :]
    t = a.shape[-1]
    mask = jnp.tril(jnp.ones((t, t), dtype=bool))
    return jnp.where(mask, diff, -jnp.inf)


def ssd_chunked_scan(x, dt, a, bm, cm):
    bsz, seq, nh, hp = x.shape
    ng, ns = bm.shape[-2:]
    r = nh // ng
    q = SSD_CHUNK
    nc = seq // q
    xd = (x.astype(jnp.float32) * dt[..., None]).reshape(bsz, nc, q, ng, r, hp)
    da = (dt * a).reshape(bsz, nc, q, ng, r).transpose(0, 3, 4, 1, 2)
    bm = bm.astype(jnp.float32).reshape(bsz, nc, q, ng, ns)
    cm = cm.astype(jnp.float32).reshape(bsz, nc, q, ng, ns)
    da_cs = jnp.cumsum(da, axis=-1)
    decay_in = jnp.exp(segsum(da))
    cb = jnp.einsum("bclgn,bcsgn->bcgls", cm, bm)
    y_diag = jnp.einsum("bcgls,bgrcls,bcsgrp->bclgrp", cb, decay_in, xd)
    decay_states = jnp.exp(da_cs[..., -1:] - da_cs)
    states = jnp.einsum("bclgn,bgrcl,bclgrp->bcgrpn", bm, decay_states, xd)
    chunk_decay = jnp.exp(da_cs[..., -1])

    def step(h, inp):
        s_c, d_c = inp
        return h * d_c[..., None, None] + s_c, h

    h0 = jnp.zeros((bsz, ng, r, hp, ns), jnp.float32)
    _, prev = lax.scan(step, h0, (states.transpose(1, 0, 2, 3, 4, 5),
                                  chunk_decay.transpose(3, 0, 1, 2)))
    prev = prev.transpose(1, 0, 2, 3, 4, 5)
    y_off = jnp.einsum("bclgn,bcgrpn,bgrcl->bclgrp", cm, prev, jnp.exp(da_cs))
    return (y_diag + y_off).reshape(bsz, seq, nh, hp)


def ssd_group(z, xbc, dt_raw, conv_w, conv_b, a_log, dt_bias, d_skip, norm_w):
    bsz, seq, _ = z.shape
    gn = SSD_GROUPS * SSD_STATE
    xbc = jax.nn.silu(causal_depthwise_conv(xbc, conv_w, conv_b))
    xs = xbc[..., :D_SSD].reshape(bsz, seq, SSD_HEADS, SSD_HEAD_DIM)
    bm = xbc[..., D_SSD:D_SSD + gn].reshape(bsz, seq, SSD_GROUPS, SSD_STATE)
    cm = xbc[..., D_SSD + gn:].reshape(bsz, seq, SSD_GROUPS, SSD_STATE)
    dt = jax.nn.softplus(dt_raw.astype(jnp.float32) + dt_bias.astype(jnp.float32))
    a = -jnp.exp(a_log.astype(jnp.float32))
    y = ssd_chunked_scan(xs, dt, a, bm, cm)
    y = y + d_skip.astype(jnp.float32)[:, None] * xs.astype(jnp.float32)
    y = y.reshape(bsz, seq, D_SSD) * jax.nn.silu(z.astype(jnp.float32))
    yg = y.reshape(bsz, seq, SSD_GROUPS, D_SSD // SSD_GROUPS)
    yg = yg * lax.rsqrt(jnp.mean(yg * yg, axis=-1, keepdims=True) + EPS)
    return (yg.reshape(bsz, seq, D_SSD) * norm_w.astype(jnp.float32)).astype(z.dtype)


def grouped_route(h2d, w_router, router_bias):
    t = h2d.shape[0]
    s = jax.nn.sigmoid((h2d @ w_router).astype(jnp.float32))
    s_sel = s + router_bias.astype(jnp.float32)
    grp = s_sel.reshape(t, N_EXPERT_GROUPS, EXPERTS_PER_GROUP)
    grp_score = jnp.sum(lax.top_k(grp, TOP_K)[0], axis=-1)
    best = jnp.argmax(grp_score, axis=-1)
    keep = jnp.arange(N_EXPERT_GROUPS)[None, :] == best[:, None]
    masked = jnp.where(keep[..., None], grp, -jnp.inf).reshape(t, N_EXPERTS)
    _, idx = lax.top_k(masked, TOP_K)
    gw = jnp.take_along_axis(s, idx, axis=-1)
    gw = gw / jnp.sum(gw, axis=-1, keepdims=True)
    return idx, gw


def routed_experts(h2d, idx, gw, w_gate, w_up, w_down):
    t, d = h2d.shape
    n_assign = t * TOP_K
    e_flat = idx.reshape(n_assign)
    tok_flat = jnp.arange(n_assign) // TOP_K
    g_flat = gw.reshape(n_assign)
    order = jnp.argsort(e_flat)
    e_sorted, tok_sorted, g_sorted = e_flat[order], tok_flat[order], g_flat[order]
    counts = jnp.bincount(e_flat, length=N_EXPERTS)
    starts = jnp.cumsum(counts) - counts
    padded = ((counts + MOE_BLOCK - 1) // MOE_BLOCK) * MOE_BLOCK
    pends = jnp.cumsum(padded)
    pstarts = pends - padded
    dest = pstarts[e_sorted] + (jnp.arange(n_assign) - starts[e_sorted])
    n_blocks = n_assign // MOE_BLOCK + N_EXPERTS
    buf = jnp.zeros((n_blocks * MOE_BLOCK, d), h2d.dtype).at[dest].set(h2d[tok_sorted])
    block_expert = jnp.minimum(
        jnp.searchsorted(pends, jnp.arange(n_blocks) * MOE_BLOCK, side="right"),
        N_EXPERTS - 1)

    def expert_block(args):
        xb, e = args
        hb = jax.nn.silu(xb @ w_gate[e]) * (xb @ w_up[e])
        return hb @ w_down[e]

    out = lax.map(expert_block, (buf.reshape(n_blocks, MOE_BLOCK, d), block_expert))
    out = out.reshape(n_blocks * MOE_BLOCK, d)
    contrib = out[dest] * g_sorted[:, None].astype(out.dtype)
    return jnp.zeros((t, d), h2d.dtype).at[tok_sorted].add(contrib)


def hybrid_layer(x, c_act, w_mod, b_mod, norm1_w, w_in, conv_dw_w, conv_dw_b,
                 conv_ln_w, conv_ln_b, ssd_conv_w, ssd_conv_b, a_log, dt_bias,
                 d_skip, ssd_norm_w, w_out, norm2_w, w_router, router_bias,
                 w_gate, w_up, w_down):
    bsz, seq, d = x.shape
    mod = c_act @ w_mod + b_mod
    sh1, sc1, g1, sh2, sc2, g2 = jnp.split(mod, 6, axis=-1)
    h = modulate(rms_norm(x, norm1_w), sh1, sc1)
    proj = h @ w_in
    cuts = [D_CONV, 2 * D_CONV, 2 * D_CONV + D_SSD, 2 * D_CONV + D_SSD + SSD_XBC]
    conv_a, conv_g, z, xbc, dt_raw = jnp.split(proj, cuts, axis=-1)
    y_conv = conformer_conv_group(conv_a, conv_g, conv_dw_w, conv_dw_b, conv_ln_w, conv_ln_b)
    y_ssd = ssd_group(z, xbc, dt_raw, ssd_conv_w, ssd_conv_b, a_log, dt_bias, d_skip, ssd_norm_w)
    mix = jnp.concatenate([y_conv, y_ssd], axis=-1) @ w_out
    x = x + g1[:, None, :] * mix
    h = modulate(rms_norm(x, norm2_w), sh2, sc2).reshape(bsz * seq, d)
    idx, gw = grouped_route(h, w_router, router_bias)
    y = routed_experts(h, idx, gw, w_gate, w_up, w_down).reshape(bsz, seq, d)
    return x + g2[:, None, :] * y


def setup_inputs(seed: int = 0) -> dict:
    key = jax.random.key(seed)
    ks = jax.random.split(key, 26)
    f32 = jnp.float32
    n = lambda k, s, sc: jax.random.normal(k, s, f32) * sc
    gn = SSD_GROUPS * SSD_STATE
    dt0 = jnp.exp(jax.random.uniform(ks[13], (DEPTH, SSD_HEADS), f32)
                  * (np.log(0.1) - np.log(1e-3)) + np.log(1e-3))
    return {
        "x": n(ks[0], (BATCH, SEQ, D_MODEL), 1.0),
        "c": n(ks[1], (BATCH, D_MODEL), 1.0),
        "w_mod": n(ks[2], (DEPTH, D_MODEL, 6 * D_MODEL), 0.5 * D_MODEL ** -0.5),
        "b_mod": n(ks[3], (DEPTH, 6 * D_MODEL), 0.02),
        "norm1_w": 1.0 + n(ks[4], (DEPTH, D_MODEL), 0.02),
        "w_in": n(ks[5], (DEPTH, D_MODEL, D_IN_TOTAL), D_MODEL ** -0.5),
        "conv_dw_w": n(ks[6], (DEPTH, CONV_WIDTH, D_CONV), CONV_WIDTH ** -0.5),
        "conv_dw_b": n(ks[7], (DEPTH, D_CONV), 0.02),
        "conv_ln_w": 1.0 + n(ks[8], (DEPTH, D_CONV), 0.02),
        "conv_ln_b": n(ks[9], (DEPTH, D_CONV), 0.02),
        "ssd_conv_w": n(ks[10], (DEPTH, SSD_CONV_WIDTH, SSD_XBC), SSD_CONV_WIDTH ** -0.5),
        "ssd_conv_b": n(ks[11], (DEPTH, SSD_XBC), 0.02),
        "a_log": jnp.log(jax.random.uniform(ks[12], (DEPTH, SSD_HEADS), f32, 1.0, 16.0)),
        "dt_bias": dt0 + jnp.log(-jnp.expm1(-dt0)),
        "d_skip": 1.0 + n(ks[14], (DEPTH, SSD_HEADS), 0.02),
        "ssd_norm_w": 1.0 + n(ks[15], (DEPTH, D_SSD), 0.02),
        "w_out": n(ks[16], (DEPTH, D_MIX, D_MODEL), D_MIX ** -0.5),
        "norm2_w": 1.0 + n(ks[17], (DEPTH, D_MODEL), 0.02),
        "w_router": n(ks[18], (D_MODEL, N_EXPERTS), D_MODEL ** -0.5),
        "router_bias": n(ks[19], (N_EXPERTS,), 0.01),
        "w_gate": n(ks[20], (DEPTH, N_EXPERTS, D_MODEL, D_EXPERT), D_MODEL ** -0.5),
        "w_up": n(ks[21], (DEPTH, N_EXPERTS, D_MODEL, D_EXPERT), D_MODEL ** -0.5),
        "w_down": n(ks[22], (DEPTH, N_EXPERTS, D_EXPERT, D_MODEL), D_EXPERT ** -0.5),
        "final_norm_w": 1.0 + n(ks[23], (D_MODEL,), 0.02),
    }


def reference(x, c, w_mod, b_mod, norm1_w, w_in, conv_dw_w, conv_dw_b, conv_ln_w,
              conv_ln_b, ssd_conv_w, ssd_conv_b, a_log, dt_bias, d_skip, ssd_norm_w,
              w_out, norm2_w, w_router, router_bias, w_gate, w_up, w_down,
              final_norm_w):
    c_act = jax.nn.silu(c)
    for i in range(DEPTH):
        x = hybrid_layer(x, c_act, w_mod[i], b_mod[i], norm1_w[i], w_in[i],
                         conv_dw_w[i], conv_dw_b[i], conv_ln_w[i], conv_ln_b[i],
                         ssd_conv_w[i], ssd_conv_b[i], a_log[i], dt_bias[i],
                         d_skip[i], ssd_norm_w[i], w_out[i], norm2_w[i],
                         w_router, router_bias, w_gate[i], w_up[i], w_down[i])
    return rms_norm(x, final_norm_w)
```

```python
import functools

import jax
import jax.numpy as jnp
from jax import lax
from jax.experimental import pallas as pl
from jax.experimental.pallas import tpu as pltpu

F32 = jnp.float32
BF16 = jnp.bfloat16
EPS = 1e-6

LANES = 128
SUBLANES = 8
VMEM_LIMIT_BYTES = 56 * 1024 * 1024

CONV_GROUP = 128
CONV_HALO = 32
SSD_HEAD_DIM = 64
SSD_GROUPS = 2
SSD_STATE = 128
SSD_CHUNK = 128
SSD_HALO = 8
N_EXPERT_GROUPS = 8
EXPERTS_PER_GROUP = 4

TM_PROJ = 512
TL_CONV = 256
CONV_ROWS = 64
TM_EXPERT = 256
TM_COMBINE = 256
TN_MOD = 1024
TN_PROJ = 512


def _cparams(sem):
    return pltpu.CompilerParams(dimension_semantics=sem, vmem_limit_bytes=VMEM_LIMIT_BYTES)


def _sigmoid(v):
    return jax.nn.sigmoid(v)


def _split2(v):
    hi = v.astype(BF16)
    lo = (v - hi.astype(F32)).astype(BF16)
    return hi, lo


def _split3(v):
    p1 = v.astype(BF16)
    r1 = v - p1.astype(F32)
    p2 = r1.astype(BF16)
    p3 = (r1 - p2.astype(F32)).astype(BF16)
    return p1, p2, p3


def _dot(a, b):
    return jnp.dot(a, b, preferred_element_type=F32)


def _dot_nt(a, b):
    return lax.dot_general(a, b, (((1,), (1,)), ((), ())), preferred_element_type=F32)


def _mod_kernel(c_ref, w_ref, b_ref, o_ref):
    c = c_ref[...]
    nb = c.shape[0]
    ca = c * _sigmoid(c)
    c_hi, c_lo = _split2(ca)
    w_hi, w_lo = _split2(w_ref[...])
    r = _dot(jnp.concatenate([c_hi, c_lo], axis=0), w_hi)
    o_ref[...] = r[:nb] + r[nb:] + _dot(c_hi, w_lo) + b_ref[...]


def _modulation(c, w_mod, b_mod):
    depth, d, n = w_mod.shape
    bsz = c.shape[0]
    tn = min(TN_MOD, n)
    return pl.pallas_call(
        _mod_kernel,
        grid=(depth, n // tn),
        in_specs=[
            pl.BlockSpec((bsz, d), lambda l, j: (0, 0)),
            pl.BlockSpec((None, d, tn), lambda l, j: (l, 0, j)),
            pl.BlockSpec((None, 1, tn), lambda l, j: (l, 0, j)),
        ],
        out_specs=pl.BlockSpec((None, bsz, tn), lambda l, j: (l, 0, j)),
        out_shape=jax.ShapeDtypeStruct((depth, bsz, n), F32),
        compiler_params=_cparams(("arbitrary", "arbitrary")),
        name="modulation",
    )(c, w_mod, b_mod.reshape(depth, 1, n))


def _rms_modulate(x, nw, sc, sh):
    ms = jnp.mean(x * x, axis=-1, keepdims=True)
    return (x * lax.rsqrt(ms + EPS) * nw) * (1.0 + sc) + sh


def _inproj_kernel(x_ref, sh_ref, sc_ref, nw_ref, w_ref, wdt_ref, o_ref, odt_ref):
    hb = _rms_modulate(x_ref[...], nw_ref[...], sc_ref[...], sh_ref[...]).astype(BF16)
    n_main = o_ref.shape[1]
    for c0 in range(0, n_main, TN_PROJ):
        o_ref[:, c0:c0 + TN_PROJ] = _dot(hb, w_ref[:, c0:c0 + TN_PROJ]).astype(BF16)
    odt_ref[...] = _dot(hb, wdt_ref[...])


def _in_proj(x2d, mod5, norm_w, w_main, w_dt, seq):
    t, d = x2d.shape
    n_main = w_main.shape[1]
    tm = min(TM_PROJ, seq)
    per_b = seq // tm
    return pl.pallas_call(
        _inproj_kernel,
        grid=(t // tm,),
        in_specs=[
            pl.BlockSpec((tm, d), lambda i: (i, 0)),
            pl.BlockSpec((None, None, 1, d), lambda i: (i // per_b, 0, 0, 0)),
            pl.BlockSpec((None, None, 1, d), lambda i: (i // per_b, 1, 0, 0)),
            pl.BlockSpec((1, d), lambda i: (0, 0)),
            pl.BlockSpec((d, n_main), lambda i: (0, 0), pipeline_mode=pl.Buffered(1)),
            pl.BlockSpec((d, LANES), lambda i: (0, 0), pipeline_mode=pl.Buffered(1)),
        ],
        out_specs=[
            pl.BlockSpec((tm, n_main), lambda i: (i, 0)),
            pl.BlockSpec((tm, LANES), lambda i: (i, 0)),
        ],
        out_shape=[
            jax.ShapeDtypeStruct((t, n_main), BF16),
            jax.ShapeDtypeStruct((t, LANES), F32),
        ],
        compiler_params=_cparams(("arbitrary",)),
        name="in_proj",
    )(x2d, mod5, mod5, norm_w.reshape(1, d), w_main, w_dt)


def _conv_kernel(a_ref, g_ref, w_ref, b_ref, lnw_ref, lnb_ref, o_ref, ubuf):
    tl, ch = o_ref.shape
    taps = w_ref.shape[0]
    i = pl.program_id(1)

    @pl.when(i == 0)
    def _():
        ubuf[0:CONV_HALO, :] = jnp.zeros((CONV_HALO, ch), F32)

    @pl.when(i > 0)
    def _():
        ubuf[0:CONV_HALO, :] = ubuf[tl:tl + CONV_HALO, :]

    ubuf[CONV_HALO:CONV_HALO + tl, :] = a_ref[...].astype(F32) * _sigmoid(g_ref[...].astype(F32))

    rows = min(CONV_ROWS, tl)
    for cb in range(ch // CONV_GROUP):
        cs = slice(cb * CONV_GROUP, (cb + 1) * CONV_GROUP)
        bias = b_ref[:, cs]
        lnw = lnw_ref[:, cs]
        lnb = lnb_ref[:, cs]
        for rs in range(tl // rows):
            r0 = CONV_HALO - (taps - 1) + rs * rows
            acc = jnp.zeros((rows, CONV_GROUP), F32)
            for k in range(taps):
                acc = acc + w_ref[k:k + 1, cs] * ubuf[r0 + k:r0 + k + rows, cs]
            acc = acc + bias
            mu = jnp.mean(acc, axis=-1, keepdims=True)
            dev = acc - mu
            var = jnp.mean(dev * dev, axis=-1, keepdims=True)
            yn = dev * lax.rsqrt(var + EPS) * lnw + lnb
            o_ref[rs * rows:(rs + 1) * rows, cs] = (yn * _sigmoid(yn)).astype(o_ref.dtype)


def _conformer_conv(proj, dw_w, dw_b, ln_w, ln_b, bsz, seq):
    taps, ch = dw_w.shape
    tl = min(TL_CONV, seq)
    nl = seq // tl
    row = lambda b, i: b * nl + i
    return pl.pallas_call(
        _conv_kernel,
        grid=(bsz, nl),
        in_specs=[
            pl.BlockSpec((tl, ch), lambda b, i: (row(b, i), 0)),
            pl.BlockSpec((tl, ch), lambda b, i: (row(b, i), 1)),
            pl.BlockSpec((taps, ch), lambda b, i: (0, 0)),
            pl.BlockSpec((1, ch), lambda b, i: (0, 0)),
            pl.BlockSpec((1, ch), lambda b, i: (0, 0)),
            pl.BlockSpec((1, ch), lambda b, i: (0, 0)),
        ],
        out_specs=pl.BlockSpec((tl, ch), lambda b, i: (row(b, i), 0)),
        out_shape=jax.ShapeDtypeStruct((bsz * seq, ch), BF16),
        scratch_shapes=[pltpu.VMEM((CONV_HALO + tl, ch), F32)],
        compiler_params=_cparams(("arbitrary", "arbitrary")),
        name="conformer_conv",
    )(proj, proj, dw_w, dw_b.reshape(1, ch), ln_w.reshape(1, ch), ln_b.reshape(1, ch))


def _ssd_kernel(z_ref, xbc_ref, dt_ref, cw_ref, cb_ref, alog_ref, dtb_ref, dskip_ref, nw_ref,
                expand_ref, o_ref, xbuf, state):
    q, d_ssd = o_ref.shape
    gw = d_ssd // SSD_GROUPS
    gn = SSD_GROUPS * SSD_STATE
    taps = cw_ref.shape[0]
    c = pl.program_id(1)

    @pl.when(c == 0)
    def _():
        xbuf[0:SSD_HALO, :] = jnp.zeros((SSD_HALO, xbuf.shape[1]), F32)
        state[...] = jnp.zeros(state.shape, F32)

    @pl.when(c > 0)
    def _():
        xbuf[0:SSD_HALO, :] = xbuf[q:q + SSD_HALO, :]

    xbuf[SSD_HALO:SSD_HALO + q, :] = xbc_ref[...].astype(F32)
    acc = jnp.zeros((q, xbuf.shape[1]), F32) + cb_ref[...]
    for k in range(taps):
        r0 = SSD_HALO - (taps - 1) + k
        acc = acc + cw_ref[k:k + 1, :] * xbuf[r0:r0 + q, :]
    xc = acc * _sigmoid(acc)
    xs = xc[:, :d_ssd]
    bm = xc[:, d_ssd:d_ssd + gn]
    cm = xc[:, d_ssd + gn:]

    dt = jax.nn.softplus(dt_ref[...] + dtb_ref[...])
    a = -jnp.exp(alog_ref[...])
    da = dt * a
    rows = lax.broadcasted_iota(jnp.int32, (q, q), 0)
    cols = lax.broadcasted_iota(jnp.int32, (q, q), 1)
    tril = rows >= cols
    tri = jnp.where(tril, 1.0, 0.0).astype(BF16)
    d1, d2, d3 = _split3(da)
    cs = _dot(tri, d1) + _dot(tri, d2) + _dot(tri, d3)
    cs_t = cs.T
    cs_last = cs[q - 1:q, :]

    expand = expand_ref[...]

    def per_channel(v):
        hi, lo = _split2(v)
        return _dot(hi, expand) + _dot(lo, expand)

    dt_e = per_channel(dt)
    ecs_e = per_channel(jnp.exp(cs))
    wds_e = per_channel(dt * jnp.exp(cs_last - cs))
    xd = (xs * dt_e).astype(BF16)
    xds = (xs * wds_e).astype(BF16)
    chunk_decay = ecs_e[q - 1:q, :]

    lane = lax.broadcasted_iota(jnp.int32, (q, LANES), 1)
    heads_per_group = gw // SSD_HEAD_DIM
    y_parts = []
    for g in range(SSD_GROUPS):
        bg = bm[:, g * SSD_STATE:(g + 1) * SSD_STATE]
        cg = cm[:, g * SSD_STATE:(g + 1) * SSD_STATE].astype(BF16)
        cb_mat = _dot_nt(cg, bg.astype(BF16))
        st = state[g]
        y_off = _dot(cg, st.astype(BF16)) * ecs_e[:, g * gw:(g + 1) * gw]
        diag_parts = []
        for pr in range(heads_per_group // 2):
            lo_col = g * gw + pr * LANES
            xd_blk = xd[:, lo_col:lo_col + LANES]
            yp = jnp.zeros((q, LANES), F32)
            for half in range(2):
                h = g * heads_per_group + 2 * pr + half
                diff = cs[:, h:h + 1] - cs_t[h:h + 1, :]
                decay = jnp.exp(jnp.where(tril, diff, -jnp.inf))
                m = (cb_mat * decay).astype(BF16)
                keep = (lane >= SSD_HEAD_DIM) if half else (lane < SSD_HEAD_DIM)
                yp = yp + _dot(m, jnp.where(keep, xd_blk, jnp.zeros_like(xd_blk)))
            diag_parts.append(yp)
        y_parts.append(jnp.concatenate(diag_parts, axis=1) + y_off)
        new = _dot(bg.T.astype(BF16), xds[:, g * gw:(g + 1) * gw])
        state[g] = st * chunk_decay[:, g * gw:(g + 1) * gw] + new

    y = jnp.concatenate(y_parts, axis=1) + dskip_ref[...] * xs
    z = z_ref[...].astype(F32)
    y = y * (z * _sigmoid(z))
    outs = []
    for g in range(SSD_GROUPS):
        yg = y[:, g * gw:(g + 1) * gw]
        outs.append(yg * lax.rsqrt(jnp.mean(yg * yg, axis=-1, keepdims=True) + EPS))
    o_ref[...] = (jnp.concatenate(outs, axis=1) * nw_ref[...]).astype(o_ref.dtype)


def _ssd(proj, dt_raw, conv_w, conv_b, a_log, dt_bias, d_skip, norm_w, bsz, seq, d_ssd, z_col0):
    taps, xbc_w = conv_w.shape
    heads = a_log.shape[0]
    q = SSD_CHUNK
    nc = seq // q
    gw = d_ssd // SSD_GROUPS
    row = lambda b, c: b * nc + c
    pad = lambda v: jnp.zeros((1, LANES), F32).at[0, :heads].set(v.astype(F32))
    head_of_ch = jnp.arange(d_ssd, dtype=jnp.int32) // SSD_HEAD_DIM
    expand = (jnp.arange(LANES, dtype=jnp.int32)[:, None] == head_of_ch[None, :]).astype(BF16)
    dskip_e = jnp.repeat(d_skip.astype(F32), SSD_HEAD_DIM).reshape(1, d_ssd)
    return pl.pallas_call(
        _ssd_kernel,
        grid=(bsz, nc),
        in_specs=[
            pl.BlockSpec((q, d_ssd), lambda b, c: (row(b, c), z_col0 // d_ssd)),
            pl.BlockSpec((q, xbc_w), lambda b, c: (row(b, c), (z_col0 + d_ssd) // xbc_w)),
            pl.BlockSpec((q, LANES), lambda b, c: (row(b, c), 0)),
            pl.BlockSpec((taps, xbc_w), lambda b, c: (0, 0)),
            pl.BlockSpec((1, xbc_w), lambda b, c: (0, 0)),
            pl.BlockSpec((1, LANES), lambda b, c: (0, 0)),
            pl.BlockSpec((1, LANES), lambda b, c: (0, 0)),
            pl.BlockSpec((1, d_ssd), lambda b, c: (0, 0)),
            pl.BlockSpec((1, d_ssd), lambda b, c: (0, 0)),
            pl.BlockSpec((LANES, d_ssd), lambda b, c: (0, 0)),
        ],
        out_specs=pl.BlockSpec((q, d_ssd), lambda b, c: (row(b, c), 0)),
        out_shape=jax.ShapeDtypeStruct((bsz * seq, d_ssd), BF16),
        scratch_shapes=[
            pltpu.VMEM((SSD_HALO + q, xbc_w), F32),
            pltpu.VMEM((SSD_GROUPS, SSD_STATE, gw), F32),
        ],
        compiler_params=_cparams(("arbitrary", "arbitrary")),
        name="ssd",
    )(proj, proj, dt_raw, conv_w, conv_b.reshape(1, xbc_w), pad(a_log), pad(dt_bias), dskip_e,
      norm_w.reshape(1, d_ssd).astype(F32), expand)


def _outproj_kernel(yc_ref, ys_ref, x_ref, g1_ref, sh2_ref, sc2_ref, nw_ref, w_ref, wr_hi_ref,
                    wr_lo_ref, rb_ref, x1_ref, h2_ref, idx_ref, gw_ref, rank_ref, cnt_ref,
                    base):
    tm = x_ref.shape[0]
    n_exp = wr_hi_ref.shape[0]
    dc = yc_ref.shape[1]
    i = pl.program_id(0)

    @pl.when(i == 0)
    def _():
        base[...] = jnp.zeros(base.shape, F32)

    mix = _dot(yc_ref[...], w_ref[0:dc, :]) + _dot(ys_ref[...], w_ref[dc:, :])
    x1 = x_ref[...] + g1_ref[...] * mix
    x1_ref[...] = x1
    h2 = _rms_modulate(x1, nw_ref[...], sc2_ref[...], sh2_ref[...])
    h2_ref[...] = h2

    h_hi, h_lo = _split2(h2)
    wr_hi = wr_hi_ref[...]
    logits = _dot_nt(wr_hi, h_hi) + _dot_nt(wr_lo_ref[...], h_hi) + _dot_nt(wr_hi, h_lo)
    s = _sigmoid(logits)
    s_sel = s + rb_ref[...]
    eiota = lax.broadcasted_iota(jnp.int32, s.shape, 0)
    group_of = lax.shift_right_logical(eiota, EXPERTS_PER_GROUP.bit_length() - 1)
    in_group = jnp.bitwise_and(eiota, EXPERTS_PER_GROUP - 1)
    pair_max = jnp.full(s.shape, -jnp.inf, F32)
    for j in range(1, EXPERTS_PER_GROUP):
        partner = pltpu.roll(s_sel, n_exp - j, 0)
        pair_max = jnp.maximum(pair_max, jnp.where(in_group < EXPERTS_PER_GROUP - j,
                                                   s_sel + partner, -jnp.inf))
    gmax = jnp.max(pair_max, axis=0, keepdims=True)
    best = jnp.min(jnp.where(pair_max == gmax, group_of, N_EXPERT_GROUPS), axis=0, keepdims=True)
    masked = jnp.where(group_of == best, s_sel, -jnp.inf)
    m1 = jnp.max(masked, axis=0, keepdims=True)
    i1 = jnp.min(jnp.where(masked == m1, eiota, n_exp), axis=0, keepdims=True)
    masked = jnp.where(eiota == i1, -jnp.inf, masked)
    m2 = jnp.max(masked, axis=0, keepdims=True)
    i2 = jnp.min(jnp.where(masked == m2, eiota, n_exp), axis=0, keepdims=True)
    oh1 = eiota == i1
    oh2 = eiota == i2
    a1 = jnp.sum(jnp.where(oh1, s, 0.0), axis=0, keepdims=True)
    a2 = jnp.sum(jnp.where(oh2, s, 0.0), axis=0, keepdims=True)
    tot = a1 + a2
    idx_ref[0:1, :] = i1
    idx_ref[1:2, :] = i2
    gw_ref[0:1, :] = a1 / tot
    gw_ref[1:2, :] = a2 / tot

    oh_any = jnp.where(oh1 | oh2, 1.0, 0.0)
    r_i = lax.broadcasted_iota(jnp.int32, (tm, tm), 0)
    c_i = lax.broadcasted_iota(jnp.int32, (tm, tm), 1)
    upper = jnp.where(r_i <= c_i, 1.0, 0.0).astype(BF16)
    incl = _dot(oh_any.astype(BF16), upper)
    rank_all = base[...] + incl - oh_any
    rank_ref[0:1, :] = jnp.sum(jnp.where(oh1, rank_all, 0.0), axis=0, keepdims=True).astype(jnp.int32)
    rank_ref[1:2, :] = jnp.sum(jnp.where(oh2, rank_all, 0.0), axis=0, keepdims=True).astype(jnp.int32)
    new_base = base[...] + jnp.sum(oh_any, axis=1, keepdims=True)
    base[...] = new_base
    cnt_ref[...] = jnp.broadcast_to(new_base, cnt_ref.shape).astype(jnp.int32)


def _out_proj(y_conv, y_ssd, x2d, mod5, norm_w, w_out, wr_hi, wr_lo, router_bias, seq):
    t, d = x2d.shape
    dc = y_conv.shape[1]
    ds = y_ssd.shape[1]
    n_exp = wr_hi.shape[0]
    tm = min(TM_PROJ, seq)
    per_b = seq // tm
    modspec = lambda j: pl.BlockSpec((None, None, 1, d), lambda i: (i // per_b, j, 0, 0))
    const = lambda shape: pl.BlockSpec(shape, lambda i: (0, 0))
    return pl.pallas_call(
        _outproj_kernel,
        grid=(t // tm,),
        in_specs=[
            pl.BlockSpec((tm, dc), lambda i: (i, 0)),
            pl.BlockSpec((tm, ds), lambda i: (i, 0)),
            pl.BlockSpec((tm, d), lambda i: (i, 0)),
            modspec(2), modspec(3), modspec(4),
            const((1, d)),
            pl.BlockSpec((dc + ds, d), lambda i: (0, 0), pipeline_mode=pl.Buffered(1)),
            const((n_exp, d)), const((n_exp, d)), const((n_exp, 1)),
        ],
        out_specs=[
            pl.BlockSpec((tm, d), lambda i: (i, 0)),
            pl.BlockSpec((tm, d), lambda i: (i, 0)),
            pl.BlockSpec((2, tm), lambda i: (0, i)),
            pl.BlockSpec((2, tm), lambda i: (0, i)),
            pl.BlockSpec((2, tm), lambda i: (0, i)),
            const((n_exp, LANES)),
        ],
        out_shape=[
            jax.ShapeDtypeStruct((t, d), F32),
            jax.ShapeDtypeStruct((t, d), F32),
            jax.ShapeDtypeStruct((2, t), jnp.int32),
            jax.ShapeDtypeStruct((2, t), F32),
            jax.ShapeDtypeStruct((2, t), jnp.int32),
            jax.ShapeDtypeStruct((n_exp, LANES), jnp.int32),
        ],
        scratch_shapes=[pltpu.VMEM((n_exp, 1), F32)],
        compiler_params=_cparams(("arbitrary",)),
        name="out_proj_router",
    )(y_conv, y_ssd, x2d, mod5, mod5, mod5, norm_w.reshape(1, d), w_out, wr_hi, wr_lo,
      router_bias.reshape(n_exp, 1).astype(F32))


def _gather_rows(idx_hbm, src_hbm, idx_smem, buf, sem, isem, blk, slot):
    n = idx_smem.shape[1]
    cp = pltpu.make_async_copy(idx_hbm.at[blk], idx_smem.at[slot], isem)
    cp.start()
    cp.wait()

    def body(r8, carry):
        for u in range(SUBLANES):
            r = r8 * SUBLANES + u
            pltpu.make_async_copy(src_hbm.at[pl.ds(idx_smem[slot, r], 1)],
                                  buf.at[slot, pl.ds(r, 1)], sem.at[slot]).start()
        return carry

    lax.fori_loop(0, n // SUBLANES, body, 0)


def _wait_rows(src_hbm, buf, sem, slot):
    n = buf.shape[1]
    pltpu.make_async_copy(src_hbm.at[pl.ds(0, n)], buf.at[slot], sem.at[slot]).wait()


def _expert_kernel(be_ref, nused_ref, stok_hbm, h_hbm, sg_ref, wg_ref, wu_ref, wd_ref, o_ref,
                   idx_smem, xbuf, sem, isem, wg_bf, wu_bf, wd_bf):
    b = pl.program_id(0)
    slot = lax.rem(b, 2)
    n_used = nused_ref[0]

    @pl.when((b == 0) & (n_used > 0))
    def _():
        _gather_rows(stok_hbm, h_hbm, idx_smem, xbuf, sem, isem, 0, 0)

    @pl.when(b + 1 < n_used)
    def _():
        _gather_rows(stok_hbm, h_hbm, idx_smem, xbuf, sem, isem, b + 1, 1 - slot)

    prev = be_ref[jnp.maximum(b - 1, 0)]

    @pl.when((b == 0) | (be_ref[b] != prev))
    def _():
        wg_bf[...] = wg_ref[...].astype(BF16)
        wu_bf[...] = wu_ref[...].astype(BF16)
        wd_bf[...] = wd_ref[...].astype(BF16)

    @pl.when(b < n_used)
    def _():
        _wait_rows(h_hbm, xbuf, sem, slot)
        x = xbuf[slot].astype(BF16)
        gate = _dot(x, wg_bf[...])
        up = _dot(x, wu_bf[...])
        hb = (gate * _sigmoid(gate) * up).astype(BF16)
        o_ref[...] = _dot(hb, wd_bf[...]) * sg_ref[...]

    @pl.when(b >= n_used)
    def _():
        o_ref[...] = jnp.zeros(o_ref.shape, o_ref.dtype)


def _experts(h2, block_expert, n_used, slot_tok, slot_gate, w_gate, w_up, w_down):
    t, d = h2.shape
    nb, tm = slot_tok.shape
    n_exp, _, de = w_gate.shape
    grid_spec = pltpu.PrefetchScalarGridSpec(
        num_scalar_prefetch=2,
        grid=(nb,),
        in_specs=[
            pl.BlockSpec(memory_space=pl.ANY),
            pl.BlockSpec(memory_space=pl.ANY),
            pl.BlockSpec((None, tm, 1), lambda b, be, nu: (b, 0, 0)),
            pl.BlockSpec((None, d, de), lambda b, be, nu: (be[b], 0, 0)),
            pl.BlockSpec((None, d, de), lambda b, be, nu: (be[b], 0, 0)),
            pl.BlockSpec((None, de, d), lambda b, be, nu: (be[b], 0, 0)),
        ],
        out_specs=pl.BlockSpec((tm, d), lambda b, be, nu: (b, 0)),
        scratch_shapes=[
            pltpu.SMEM((2, tm), jnp.int32),
            pltpu.VMEM((2, tm, d), F32),
            pltpu.SemaphoreType.DMA((2,)),
            pltpu.SemaphoreType.DMA(()),
            pltpu.VMEM((d, de), BF16),
            pltpu.VMEM((d, de), BF16),
            pltpu.VMEM((de, d), BF16),
        ],
    )
    return pl.pallas_call(
        _expert_kernel,
        grid_spec=grid_spec,
        out_shape=jax.ShapeDtypeStruct((nb * tm, d), F32),
        compiler_params=_cparams(("arbitrary",)),
        name="experts",
    )(block_expert, n_used, slot_tok, h2, slot_gate.reshape(nb, tm, 1), w_gate, w_up, w_down)


def _combine_kernel(final, pos_hbm, ys_hbm, x1_ref, g2_ref, fnw_ref, o_ref, idx_smem, ybuf, sem, isem):
    i = pl.program_id(0)
    n = pl.num_programs(0)
    slot = lax.rem(i, 2)
    tm = x1_ref.shape[0]

    @pl.when(i == 0)
    def _():
        _gather_rows(pos_hbm, ys_hbm, idx_smem, ybuf, sem, isem, 0, 0)

    @pl.when(i + 1 < n)
    def _():
        _gather_rows(pos_hbm, ys_hbm, idx_smem, ybuf, sem, isem, i + 1, 1 - slot)

    _wait_rows(ys_hbm, ybuf, sem, slot)
    y = ybuf[slot, 0:tm, :] + ybuf[slot, tm:2 * tm, :]
    x2 = x1_ref[...] + g2_ref[...] * y
    if final:
        ms = jnp.mean(x2 * x2, axis=-1, keepdims=True)
        x2 = x2 * lax.rsqrt(ms + EPS) * fnw_ref[...]
    o_ref[...] = x2


def _combine(pos_tiles, ys, x1, mod5, final_norm_w, seq, final):
    t, d = x1.shape
    nt, two_tm = pos_tiles.shape
    tm = two_tm // 2
    per_b = seq // tm
    return pl.pallas_call(
        functools.partial(_combine_kernel, final),
        grid=(nt,),
        in_specs=[
            pl.BlockSpec(memory_space=pl.ANY),
            pl.BlockSpec(memory_space=pl.ANY),
            pl.BlockSpec((tm, d), lambda i: (i, 0)),
            pl.BlockSpec((None, None, 1, d), lambda i: (i // per_b, 5, 0, 0)),
            pl.BlockSpec((1, d), lambda i: (0, 0)),
        ],
        out_specs=pl.BlockSpec((tm, d), lambda i: (i, 0)),
        out_shape=jax.ShapeDtypeStruct((t, d), F32),
        scratch_shapes=[
            pltpu.SMEM((2, two_tm), jnp.int32),
            pltpu.VMEM((2, two_tm, d), F32),
            pltpu.SemaphoreType.DMA((2,)),
            pltpu.SemaphoreType.DMA(()),
        ],
        compiler_params=_cparams(("arbitrary",)),
        name="combine",
    )(pos_tiles, ys, x1, mod5, final_norm_w.reshape(1, d).astype(F32))


def _routing_tables(idx, gw, rank, counts, tm_e, tm_c):
    n_exp = counts.shape[0]
    t = idx.shape[1]
    nb = (2 * t) // tm_e + n_exp
    padded = ((counts + tm_e - 1) // tm_e) * tm_e
    pends = jnp.cumsum(padded)
    pstarts = pends - padded
    pos = pstarts[idx] + rank
    n_used = (pends[-1] // tm_e).astype(jnp.int32).reshape(1)
    block_expert = jnp.minimum(
        jnp.searchsorted(pends, jnp.arange(nb, dtype=jnp.int32) * tm_e, side="right"),
        n_exp - 1).astype(jnp.int32)
    tok = jnp.broadcast_to(jnp.arange(t, dtype=jnp.int32)[None, :], (2, t))
    flat = pos.reshape(-1)
    slot_tok = jnp.zeros((nb * tm_e,), jnp.int32).at[flat].set(tok.reshape(-1))
    slot_gate = jnp.zeros((nb * tm_e,), F32).at[flat].set(gw.reshape(-1))
    nt = t // tm_c
    pos_tiles = pos.reshape(2, nt, tm_c).transpose(1, 0, 2).reshape(nt, 2 * tm_c)
    return block_expert, n_used, slot_tok.reshape(nb, tm_e), slot_gate.reshape(nb, tm_e), pos_tiles


def kernel(x, c, w_mod, b_mod, norm1_w, w_in, conv_dw_w, conv_dw_b, conv_ln_w, conv_ln_b,
           ssd_conv_w, ssd_conv_b, a_log, dt_bias, d_skip, ssd_norm_w, w_out, norm2_w, w_router,
           router_bias, w_gate, w_up, w_down, final_norm_w):
    bsz, seq, d = x.shape
    depth = w_mod.shape[0]
    t = bsz * seq
    d_conv = conv_dw_w.shape[2]
    d_ssd = ssd_norm_w.shape[1]
    heads = a_log.shape[1]
    n_main = w_in.shape[2] - heads
    assert seq % SSD_CHUNK == 0 and n_main % TN_PROJ == 0 and d_conv % CONV_GROUP == 0
    assert 2 * d_conv % d_ssd == 0 and (2 * d_conv + d_ssd) % ssd_conv_w.shape[2] == 0

    mod = _modulation(c, w_mod, b_mod).reshape(depth, bsz, 6, 1, d)
    wr_hi, wr_lo = _split2(w_router.T)
    tm_e = min(TM_EXPERT, seq)
    tm_c = min(TM_COMBINE, seq)

    x2d = x.reshape(t, d)
    for l in range(depth):
        w_main = w_in[l, :, :n_main].astype(BF16)
        w_dt = jnp.zeros((d, LANES), BF16).at[:, :heads].set(w_in[l, :, n_main:].astype(BF16))
        proj, dt_raw = _in_proj(x2d, mod[l], norm1_w[l], w_main, w_dt, seq)
        y_conv = _conformer_conv(proj, conv_dw_w[l], conv_dw_b[l], conv_ln_w[l], conv_ln_b[l], bsz, seq)
        y_ssd = _ssd(proj, dt_raw, ssd_conv_w[l], ssd_conv_b[l], a_log[l], dt_bias[l], d_skip[l],
                     ssd_norm_w[l], bsz, seq, d_ssd, 2 * d_conv)
        x1, h2, idx, gw, rank, cnt = _out_proj(y_conv, y_ssd, x2d, mod[l], norm2_w[l],
                                               w_out[l].astype(BF16), wr_hi, wr_lo, router_bias, seq)
        block_expert, n_used, slot_tok, slot_gate, pos_tiles = _routing_tables(
            idx, gw, rank, cnt[:, 0], tm_e, tm_c)
        ys = _experts(h2, block_expert, n_used, slot_tok, slot_gate, w_gate[l], w_up[l], w_down[l])
        x2d = _combine(pos_tiles, ys, x1, mod[l], final_norm_w, seq, final=(l == depth - 1))
    return x2d.reshape(bsz, seq, d)
```

```python
import functools

import jax
import jax.numpy as jnp
from jax import lax
from jax.experimental import pallas as pl
from jax.experimental.pallas import tpu as pltpu

F32 = jnp.float32
BF16 = jnp.bfloat16
EPS = 1e-6

LANES = 128
LANE_SHIFT = LANES.bit_length() - 1
SUBLANES = 8
VMEM_LIMIT_BYTES = 56 * 1024 * 1024

CONV_GROUP = 128
CONV_HALO = 32
SSD_HEAD_DIM = 64
SSD_GROUPS = 2
SSD_STATE = 128
SSD_CHUNK = 128
SSD_HALO = 8
N_EXPERT_GROUPS = 8
EXPERTS_PER_GROUP = 4

TM_PROJ = 512
TL_CONV = 256
CONV_ROWS = 64
TM_EXPERT = 256
TM_COMBINE = 256
TN_MOD = 1024
TN_PROJ = 512
DMA_UNROLL = 16


def _cparams(sem):
    return pltpu.CompilerParams(dimension_semantics=sem, vmem_limit_bytes=VMEM_LIMIT_BYTES)


def _sigmoid(v):
    return jax.nn.sigmoid(v)


def _split2(v):
    hi = v.astype(BF16)
    lo = (v - hi.astype(F32)).astype(BF16)
    return hi, lo


def _split3(v):
    p1 = v.astype(BF16)
    r1 = v - p1.astype(F32)
    p2 = r1.astype(BF16)
    p3 = (r1 - p2.astype(F32)).astype(BF16)
    return p1, p2, p3


def _dot(a, b):
    return jnp.dot(a, b, preferred_element_type=F32)


def _dot_nt(a, b):
    return lax.dot_general(a, b, (((1,), (1,)), ((), ())), preferred_element_type=F32)


def _mod_kernel(c_ref, w_ref, b_ref, o_ref):
    c = c_ref[...]
    nb = c.shape[0]
    ca = c * _sigmoid(c)
    c_hi, c_lo = _split2(ca)
    w_hi, w_lo = _split2(w_ref[...])
    r = _dot(jnp.concatenate([c_hi, c_lo], axis=0), w_hi)
    o_ref[...] = r[:nb] + r[nb:] + _dot(c_hi, w_lo) + b_ref[...]


def _modulation(c, w_mod, b_mod):
    depth, d, n = w_mod.shape
    bsz = c.shape[0]
    tn = min(TN_MOD, n)
    return pl.pallas_call(
        _mod_kernel,
        grid=(depth, n // tn),
        in_specs=[
            pl.BlockSpec((bsz, d), lambda l, j: (0, 0)),
            pl.BlockSpec((None, d, tn), lambda l, j: (l, 0, j)),
            pl.BlockSpec((None, 1, tn), lambda l, j: (l, 0, j)),
        ],
        out_specs=pl.BlockSpec((None, bsz, tn), lambda l, j: (l, 0, j)),
        out_shape=jax.ShapeDtypeStruct((depth, bsz, n), F32),
        compiler_params=_cparams(("arbitrary", "arbitrary")),
        name="modulation",
    )(c, w_mod, b_mod.reshape(depth, 1, n))


def _mod_spec(l, j, per_b, d):
    return pl.BlockSpec((None, None, None, 1, d), lambda i: (l, i // per_b, j, 0, 0))


def _rms_modulate(x, nw, sc, sh):
    ms = jnp.mean(x * x, axis=-1, keepdims=True)
    return (x * lax.rsqrt(ms + EPS) * nw) * (1.0 + sc) + sh


def _inproj_kernel(x_ref, sh_ref, sc_ref, nw_ref, w_ref, wdt_ref, o_ref, odt_ref):
    hb = _rms_modulate(x_ref[...], nw_ref[...], sc_ref[...], sh_ref[...]).astype(BF16)
    n_main = o_ref.shape[1]
    for c0 in range(0, n_main, TN_PROJ):
        o_ref[:, c0:c0 + TN_PROJ] = _dot(hb, w_ref[:, c0:c0 + TN_PROJ]).astype(BF16)
    odt_ref[...] = _dot(hb, wdt_ref[...])


def _in_proj(l, x2d, mod, norm_w, w_in, w_dt, n_main, seq):
    t, d = x2d.shape
    tm = min(TM_PROJ, seq)
    per_b = seq // tm
    return pl.pallas_call(
        _inproj_kernel,
        grid=(t // tm,),
        in_specs=[
            pl.BlockSpec((tm, d), lambda i: (i, 0)),
            _mod_spec(l, 0, per_b, d),
            _mod_spec(l, 1, per_b, d),
            pl.BlockSpec((None, 1, d), lambda i: (l, 0, 0)),
            pl.BlockSpec((None, d, n_main), lambda i: (l, 0, 0), pipeline_mode=pl.Buffered(1)),
            pl.BlockSpec((None, d, LANES), lambda i: (l, 0, 0), pipeline_mode=pl.Buffered(1)),
        ],
        out_specs=[
            pl.BlockSpec((tm, n_main), lambda i: (i, 0)),
            pl.BlockSpec((tm, LANES), lambda i: (i, 0)),
        ],
        out_shape=[
            jax.ShapeDtypeStruct((t, n_main), BF16),
            jax.ShapeDtypeStruct((t, LANES), F32),
        ],
        compiler_params=_cparams(("arbitrary",)),
        name="in_proj",
    )(x2d, mod, mod, norm_w, w_in, w_dt)


def _conv_kernel(a_ref, g_ref, w_ref, b_ref, lnw_ref, lnb_ref, o_ref, ubuf):
    tl, ch = o_ref.shape
    taps = w_ref.shape[0]
    i = pl.program_id(1)

    @pl.when(i == 0)
    def _():
        ubuf[0:CONV_HALO, :] = jnp.zeros((CONV_HALO, ch), F32)

    @pl.when(i > 0)
    def _():
        ubuf[0:CONV_HALO, :] = ubuf[tl:tl + CONV_HALO, :]

    ubuf[CONV_HALO:CONV_HALO + tl, :] = a_ref[...].astype(F32) * _sigmoid(g_ref[...].astype(F32))

    rows = min(CONV_ROWS, tl)
    for cb in range(ch // CONV_GROUP):
        cs = slice(cb * CONV_GROUP, (cb + 1) * CONV_GROUP)
        bias = b_ref[:, cs]
        lnw = lnw_ref[:, cs]
        lnb = lnb_ref[:, cs]
        for rs in range(tl // rows):
            r0 = CONV_HALO - (taps - 1) + rs * rows
            acc = jnp.zeros((rows, CONV_GROUP), F32)
            for k in range(taps):
                acc = acc + w_ref[k:k + 1, cs] * ubuf[r0 + k:r0 + k + rows, cs]
            acc = acc + bias
            mu = jnp.mean(acc, axis=-1, keepdims=True)
            dev = acc - mu
            var = jnp.mean(dev * dev, axis=-1, keepdims=True)
            yn = dev * lax.rsqrt(var + EPS) * lnw + lnb
            o_ref[rs * rows:(rs + 1) * rows, cs] = (yn * _sigmoid(yn)).astype(o_ref.dtype)


def _conformer_conv(l, proj, dw_w, dw_b, ln_w, ln_b, bsz, seq):
    _, taps, ch = dw_w.shape
    tl = min(TL_CONV, seq)
    nl = seq // tl
    row = lambda b, i: b * nl + i
    vec = pl.BlockSpec((None, 1, ch), lambda b, i: (l, 0, 0))
    return pl.pallas_call(
        _conv_kernel,
        grid=(bsz, nl),
        in_specs=[
            pl.BlockSpec((tl, ch), lambda b, i: (row(b, i), 0)),
            pl.BlockSpec((tl, ch), lambda b, i: (row(b, i), 1)),
            pl.BlockSpec((None, taps, ch), lambda b, i: (l, 0, 0)),
            vec, vec, vec,
        ],
        out_specs=pl.BlockSpec((tl, ch), lambda b, i: (row(b, i), 0)),
        out_shape=jax.ShapeDtypeStruct((bsz * seq, ch), BF16),
        scratch_shapes=[pltpu.VMEM((CONV_HALO + tl, ch), F32)],
        compiler_params=_cparams(("arbitrary", "arbitrary")),
        name="conformer_conv",
    )(proj, proj, dw_w, dw_b, ln_w, ln_b)


def _ssd_kernel(z_ref, xbc_ref, dt_ref, cw_ref, cb_ref, alog_ref, dtb_ref, dskip_ref, nw_ref,
                expand_ref, o_ref, xbuf, state):
    q, d_ssd = o_ref.shape
    gw = d_ssd // SSD_GROUPS
    gn = SSD_GROUPS * SSD_STATE
    taps = cw_ref.shape[0]
    c = pl.program_id(1)

    @pl.when(c == 0)
    def _():
        xbuf[0:SSD_HALO, :] = jnp.zeros((SSD_HALO, xbuf.shape[1]), F32)
        state[...] = jnp.zeros(state.shape, F32)

    @pl.when(c > 0)
    def _():
        xbuf[0:SSD_HALO, :] = xbuf[q:q + SSD_HALO, :]

    xbuf[SSD_HALO:SSD_HALO + q, :] = xbc_ref[...].astype(F32)
    acc = jnp.zeros((q, xbuf.shape[1]), F32) + cb_ref[...]
    for k in range(taps):
        r0 = SSD_HALO - (taps - 1) + k
        acc = acc + cw_ref[k:k + 1, :] * xbuf[r0:r0 + q, :]
    xc = acc * _sigmoid(acc)
    xs = xc[:, :d_ssd]
    bm = xc[:, d_ssd:d_ssd + gn]
    cm = xc[:, d_ssd + gn:]

    dt = jax.nn.softplus(dt_ref[...] + dtb_ref[...])
    a = -jnp.exp(alog_ref[...])
    da = dt * a
    rows = lax.broadcasted_iota(jnp.int32, (q, q), 0)
    cols = lax.broadcasted_iota(jnp.int32, (q, q), 1)
    tril = rows >= cols
    tri = jnp.where(tril, 1.0, 0.0).astype(BF16)
    d1, d2, d3 = _split3(da)
    cs = _dot(tri, d1) + _dot(tri, d2) + _dot(tri, d3)
    cs_t = cs.T
    cs_last = cs[q - 1:q, :]

    expand = expand_ref[...]

    def per_channel(v):
        hi, lo = _split2(v)
        return _dot(hi, expand) + _dot(lo, expand)

    dt_e = per_channel(dt)
    ecs_e = per_channel(jnp.exp(cs))
    wds_e = per_channel(dt * jnp.exp(cs_last - cs))
    xd = (xs * dt_e).astype(BF16)
    xds = (xs * wds_e).astype(BF16)
    chunk_decay = ecs_e[q - 1:q, :]

    lane = lax.broadcasted_iota(jnp.int32, (q, LANES), 1)
    heads_per_group = gw // SSD_HEAD_DIM
    y_parts = []
    for g in range(SSD_GROUPS):
        bg = bm[:, g * SSD_STATE:(g + 1) * SSD_STATE]
        cg = cm[:, g * SSD_STATE:(g + 1) * SSD_STATE].astype(BF16)
        cb_mat = _dot_nt(cg, bg.astype(BF16))
        st = state[g]
        y_off = _dot(cg, st.astype(BF16)) * ecs_e[:, g * gw:(g + 1) * gw]
        diag_parts = []
        for pr in range(heads_per_group // 2):
            lo_col = g * gw + pr * LANES
            xd_blk = xd[:, lo_col:lo_col + LANES]
            yp = jnp.zeros((q, LANES), F32)
            for half in range(2):
                h = g * heads_per_group + 2 * pr + half
                diff = cs[:, h:h + 1] - cs_t[h:h + 1, :]
                decay = jnp.exp(jnp.where(tril, diff, -jnp.inf))
                m = (cb_mat * decay).astype(BF16)
                keep = (lane >= SSD_HEAD_DIM) if half else (lane < SSD_HEAD_DIM)
                yp = yp + _dot(m, jnp.where(keep, xd_blk, jnp.zeros_like(xd_blk)))
            diag_parts.append(yp)
        y_parts.append(jnp.concatenate(diag_parts, axis=1) + y_off)
        new = _dot(bg.T.astype(BF16), xds[:, g * gw:(g + 1) * gw])
        state[g] = st * chunk_decay[:, g * gw:(g + 1) * gw] + new

    y = jnp.concatenate(y_parts, axis=1) + dskip_ref[...] * xs
    z = z_ref[...].astype(F32)
    y = y * (z * _sigmoid(z))
    outs = []
    for g in range(SSD_GROUPS):
        yg = y[:, g * gw:(g + 1) * gw]
        outs.append(yg * lax.rsqrt(jnp.mean(yg * yg, axis=-1, keepdims=True) + EPS))
    o_ref[...] = (jnp.concatenate(outs, axis=1) * nw_ref[...]).astype(o_ref.dtype)


def _ssd(l, proj, dt_raw, conv_w, conv_b, a_log, dt_bias, dskip_e, norm_w, expand, bsz, seq, z_col0):
    _, taps, xbc_w = conv_w.shape
    d_ssd = norm_w.shape[2]
    q = SSD_CHUNK
    nc = seq // q
    gw = d_ssd // SSD_GROUPS
    row = lambda b, c: b * nc + c
    vec = lambda n: pl.BlockSpec((None, 1, n), lambda b, c: (l, 0, 0))
    return pl.pallas_call(
        _ssd_kernel,
        grid=(bsz, nc),
        in_specs=[
            pl.BlockSpec((q, d_ssd), lambda b, c: (row(b, c), z_col0 // d_ssd)),
            pl.BlockSpec((q, xbc_w), lambda b, c: (row(b, c), (z_col0 + d_ssd) // xbc_w)),
            pl.BlockSpec((q, LANES), lambda b, c: (row(b, c), 0)),
            pl.BlockSpec((None, taps, xbc_w), lambda b, c: (l, 0, 0)),
            vec(xbc_w), vec(LANES), vec(LANES), vec(d_ssd), vec(d_ssd),
            pl.BlockSpec((LANES, d_ssd), lambda b, c: (0, 0)),
        ],
        out_specs=pl.BlockSpec((q, d_ssd), lambda b, c: (row(b, c), 0)),
        out_shape=jax.ShapeDtypeStruct((bsz * seq, d_ssd), BF16),
        scratch_shapes=[
            pltpu.VMEM((SSD_HALO + q, xbc_w), F32),
            pltpu.VMEM((SSD_GROUPS, SSD_STATE, gw), F32),
        ],
        compiler_params=_cparams(("arbitrary", "arbitrary")),
        name="ssd",
    )(proj, proj, dt_raw, conv_w, conv_b, a_log, dt_bias, dskip_e, norm_w, expand)


def _outproj_kernel(yc_ref, ys_ref, x_ref, g1_ref, sh2_ref, sc2_ref, nw_ref, w_ref, wr_hi_ref,
                    wr_lo_ref, rb_ref, x1_ref, h2_ref, idx_ref, gwt_ref, rank_ref, cnt_ref, base):
    tm = x_ref.shape[0]
    n_exp = wr_hi_ref.shape[0]
    dc = yc_ref.shape[1]
    i = pl.program_id(0)

    @pl.when(i == 0)
    def _():
        base[...] = jnp.zeros(base.shape, F32)

    mix = _dot(yc_ref[...], w_ref[0:dc, :]) + _dot(ys_ref[...], w_ref[dc:, :])
    x1 = x_ref[...] + g1_ref[...] * mix
    x1_ref[...] = x1
    h2 = _rms_modulate(x1, nw_ref[...], sc2_ref[...], sh2_ref[...])
    h2_ref[...] = h2

    h_hi, h_lo = _split2(h2)
    wr_hi = wr_hi_ref[...]
    logits = _dot_nt(wr_hi, h_hi) + _dot_nt(wr_lo_ref[...], h_hi) + _dot_nt(wr_hi, h_lo)
    s = _sigmoid(logits)
    s_sel = s + rb_ref[...]
    eiota = lax.broadcasted_iota(jnp.int32, s.shape, 0)
    group_of = lax.shift_right_logical(eiota, EXPERTS_PER_GROUP.bit_length() - 1)
    in_group = jnp.bitwise_and(eiota, EXPERTS_PER_GROUP - 1)
    pair_max = jnp.full(s.shape, -jnp.inf, F32)
    for j in range(1, EXPERTS_PER_GROUP):
        partner = pltpu.roll(s_sel, n_exp - j, 0)
        pair_max = jnp.maximum(pair_max, jnp.where(in_group < EXPERTS_PER_GROUP - j,
                                                   s_sel + partner, -jnp.inf))
    gmax = jnp.max(pair_max, axis=0, keepdims=True)
    best = jnp.min(jnp.where(pair_max == gmax, group_of, N_EXPERT_GROUPS), axis=0, keepdims=True)
    masked = jnp.where(group_of == best, s_sel, -jnp.inf)
    m1 = jnp.max(masked, axis=0, keepdims=True)
    i1 = jnp.min(jnp.where(masked == m1, eiota, n_exp), axis=0, keepdims=True)
    masked = jnp.where(eiota == i1, -jnp.inf, masked)
    m2 = jnp.max(masked, axis=0, keepdims=True)
    i2 = jnp.min(jnp.where(masked == m2, eiota, n_exp), axis=0, keepdims=True)
    oh1 = eiota == i1
    oh2 = eiota == i2
    a1 = jnp.sum(jnp.where(oh1, s, 0.0), axis=0, keepdims=True)
    a2 = jnp.sum(jnp.where(oh2, s, 0.0), axis=0, keepdims=True)
    tot = a1 + a2
    idx_ref[0:1, :] = i1
    idx_ref[1:2, :] = i2
    liota = lax.broadcasted_iota(jnp.int32, (LANES, tm), 0)
    gates = jnp.where(liota == 0, a1 / tot, jnp.where(liota == 1, a2 / tot, 0.0))
    gwt_ref[...] = gates.T

    oh_any = jnp.where(oh1 | oh2, 1.0, 0.0)
    r_i = lax.broadcasted_iota(jnp.int32, (tm, tm), 0)
    c_i = lax.broadcasted_iota(jnp.int32, (tm, tm), 1)
    upper = jnp.where(r_i <= c_i, 1.0, 0.0).astype(BF16)
    incl = _dot(oh_any.astype(BF16), upper)
    rank_all = base[...] + incl - oh_any
    rank_ref[0:1, :] = jnp.sum(jnp.where(oh1, rank_all, 0.0), axis=0, keepdims=True).astype(jnp.int32)
    rank_ref[1:2, :] = jnp.sum(jnp.where(oh2, rank_all, 0.0), axis=0, keepdims=True).astype(jnp.int32)
    new_base = base[...] + jnp.sum(oh_any, axis=1, keepdims=True)
    base[...] = new_base
    cnt_ref[...] = jnp.broadcast_to(new_base, cnt_ref.shape).astype(jnp.int32)


def _out_proj(l, y_conv, y_ssd, x2d, mod, norm_w, w_out, wr_hi, wr_lo, router_bias, seq):
    t, d = x2d.shape
    dc = y_conv.shape[1]
    ds = y_ssd.shape[1]
    n_exp = wr_hi.shape[0]
    tm = min(TM_PROJ, seq)
    per_b = seq // tm
    const = lambda shape: pl.BlockSpec(shape, lambda i: (0, 0))
    return pl.pallas_call(
        _outproj_kernel,
        grid=(t // tm,),
        in_specs=[
            pl.BlockSpec((tm, dc), lambda i: (i, 0)),
            pl.BlockSpec((tm, ds), lambda i: (i, 0)),
            pl.BlockSpec((tm, d), lambda i: (i, 0)),
            _mod_spec(l, 2, per_b, d), _mod_spec(l, 3, per_b, d), _mod_spec(l, 4, per_b, d),
            pl.BlockSpec((None, 1, d), lambda i: (l, 0, 0)),
            pl.BlockSpec((None, dc + ds, d), lambda i: (l, 0, 0), pipeline_mode=pl.Buffered(1)),
            const((n_exp, d)), const((n_exp, d)), const((n_exp, 1)),
        ],
        out_specs=[
            pl.BlockSpec((tm, d), lambda i: (i, 0)),
            pl.BlockSpec((tm, d), lambda i: (i, 0)),
            pl.BlockSpec((2, tm), lambda i: (0, i)),
            pl.BlockSpec((tm, LANES), lambda i: (i, 0)),
            pl.BlockSpec((2, tm), lambda i: (0, i)),
            const((n_exp, LANES)),
        ],
        out_shape=[
            jax.ShapeDtypeStruct((t, d), F32),
            jax.ShapeDtypeStruct((t, d), F32),
            jax.ShapeDtypeStruct((2, t), jnp.int32),
            jax.ShapeDtypeStruct((t, LANES), F32),
            jax.ShapeDtypeStruct((2, t), jnp.int32),
            jax.ShapeDtypeStruct((n_exp, LANES), jnp.int32),
        ],
        scratch_shapes=[pltpu.VMEM((n_exp, 1), F32)],
        compiler_params=_cparams(("arbitrary",)),
        name="out_proj_router",
    )(y_conv, y_ssd, x2d, mod, mod, mod, norm_w, w_out, wr_hi, wr_lo,
      router_bias.reshape(n_exp, 1).astype(F32))


def _gather_rows(idx_hbm, src_hbm, idx_smem, buf, sem, isem, row0, delta, slot):
    n = buf.shape[1]
    cp = pltpu.make_async_copy(idx_hbm.at[pl.ds(row0, idx_smem.shape[1])], idx_smem.at[slot], isem)
    cp.start()
    cp.wait()

    def body(it, carry):
        for u in range(DMA_UNROLL):
            r = it * DMA_UNROLL + u
            j = delta + r
            src_row = idx_smem[slot, lax.shift_right_logical(j, LANE_SHIFT), jnp.bitwise_and(j, LANES - 1)]
            pltpu.make_async_copy(src_hbm.at[pl.ds(src_row, 1)], buf.at[slot, pl.ds(r, 1)],
                                  sem.at[slot]).start()
        return carry

    lax.fori_loop(0, n // DMA_UNROLL, body, 0)


def _wait_rows(src_hbm, buf, sem, slot):
    n = buf.shape[1]
    pltpu.make_async_copy(src_hbm.at[pl.ds(0, n)], buf.at[slot], sem.at[slot]).wait()


def _expert_kernel(be_ref, src_ref, nused_ref, tok_hbm, h_hbm, wg_ref, wu_ref, wd_ref, o_ref,
                   idx_smem, xbuf, sem, isem, wg_bf, wu_bf, wd_bf):
    b = pl.program_id(0)
    slot = lax.rem(b, 2)
    n_used = nused_ref[0]

    def start(blk, s):
        src = src_ref[blk]
        _gather_rows(tok_hbm, h_hbm, idx_smem, xbuf, sem, isem,
                     lax.shift_right_logical(src, LANE_SHIFT), jnp.bitwise_and(src, LANES - 1), s)

    @pl.when((b == 0) & (n_used > 0))
    def _():
        start(0, 0)

    @pl.when(b + 1 < n_used)
    def _():
        start(b + 1, 1 - slot)

    prev = be_ref[jnp.maximum(b - 1, 0)]

    @pl.when((b == 0) | (be_ref[b] != prev))
    def _():
        wg_bf[...] = wg_ref[...].astype(BF16)
        wu_bf[...] = wu_ref[...].astype(BF16)
        wd_bf[...] = wd_ref[...].astype(BF16)

    @pl.when(b < n_used)
    def _():
        _wait_rows(h_hbm, xbuf, sem, slot)
        x = xbuf[slot].astype(BF16)
        gate = _dot(x, wg_bf[...])
        up = _dot(x, wu_bf[...])
        hb = (gate * _sigmoid(gate) * up).astype(BF16)
        o_ref[...] = _dot(hb, wd_bf[...])

    @pl.when(b >= n_used)
    def _():
        o_ref[...] = jnp.zeros(o_ref.shape, o_ref.dtype)


def _experts(l, h2, block_expert, block_src, n_used, tok_rows, w_gate, w_up, w_down, tm):
    t, d = h2.shape
    nb = block_expert.shape[0]
    _, n_exp, _, de = w_gate.shape
    wspec = lambda a, b_: pl.BlockSpec((None, None, a, b_), lambda b, be, src, nu: (l, be[b], 0, 0))
    grid_spec = pltpu.PrefetchScalarGridSpec(
        num_scalar_prefetch=3,
        grid=(nb,),
        in_specs=[
            pl.BlockSpec(memory_space=pl.ANY),
            pl.BlockSpec(memory_space=pl.ANY),
            wspec(d, de), wspec(d, de), wspec(de, d),
        ],
        out_specs=pl.BlockSpec((tm, d), lambda b, be, src, nu: (b, 0)),
        scratch_shapes=[
            pltpu.SMEM((2, tm // LANES + 1, LANES), jnp.int32),
            pltpu.VMEM((2, tm, d), F32),
            pltpu.SemaphoreType.DMA((2,)),
            pltpu.SemaphoreType.DMA(()),
            pltpu.VMEM((d, de), BF16),
            pltpu.VMEM((d, de), BF16),
            pltpu.VMEM((de, d), BF16),
        ],
    )
    return pl.pallas_call(
        _expert_kernel,
        grid_spec=grid_spec,
        out_shape=jax.ShapeDtypeStruct((nb * tm, d), F32),
        compiler_params=_cparams(("arbitrary",)),
        name="experts",
    )(block_expert, block_src, n_used, tok_rows, h2, w_gate, w_up, w_down)


def _combine_kernel(final, pos_hbm, ys_hbm, x1_ref, gwt_ref, g2_ref, fnw_ref, o_ref,
                    idx_smem, ybuf, sem, isem):
    i = pl.program_id(0)
    n = pl.num_programs(0)
    slot = lax.rem(i, 2)
    tm = x1_ref.shape[0]
    rows_per_tile = idx_smem.shape[1]

    @pl.when(i == 0)
    def _():
        _gather_rows(pos_hbm, ys_hbm, idx_smem, ybuf, sem, isem, 0, 0, 0)

    @pl.when(i + 1 < n)
    def _():
        _gather_rows(pos_hbm, ys_hbm, idx_smem, ybuf, sem, isem, (i + 1) * rows_per_tile, 0, 1 - slot)

    _wait_rows(ys_hbm, ybuf, sem, slot)
    gates = gwt_ref[...]
    y = gates[:, 0:1] * ybuf[slot, 0:tm, :] + gates[:, 1:2] * ybuf[slot, tm:2 * tm, :]
    x2 = x1_ref[...] + g2_ref[...] * y
    if final:
        ms = jnp.mean(x2 * x2, axis=-1, keepdims=True)
        x2 = x2 * lax.rsqrt(ms + EPS) * fnw_ref[...]
    o_ref[...] = x2


def _combine(l, pos_rows, ys, x1, gwt, mod, final_norm_w, seq, tm, final):
    t, d = x1.shape
    nt = t // tm
    per_b = seq // tm
    return pl.pallas_call(
        functools.partial(_combine_kernel, final),
        grid=(nt,),
        in_specs=[
            pl.BlockSpec(memory_space=pl.ANY),
            pl.BlockSpec(memory_space=pl.ANY),
            pl.BlockSpec((tm, d), lambda i: (i, 0)),
            pl.BlockSpec((tm, LANES), lambda i: (i, 0)),
            _mod_spec(l, 5, per_b, d),
            pl.BlockSpec((1, d), lambda i: (0, 0)),
        ],
        out_specs=pl.BlockSpec((tm, d), lambda i: (i, 0)),
        out_shape=jax.ShapeDtypeStruct((t, d), F32),
        scratch_shapes=[
            pltpu.SMEM((2, 2 * tm // LANES, LANES), jnp.int32),
            pltpu.VMEM((2, 2 * tm, d), F32),
            pltpu.SemaphoreType.DMA((2,)),
            pltpu.SemaphoreType.DMA(()),
        ],
        compiler_params=_cparams(("arbitrary",)),
        name="combine",
    )(pos_rows, ys, x1, gwt, mod, final_norm_w.reshape(1, d).astype(F32))


def _routing_tables(idx, rank, counts, tm_e, tm_c):
    n_exp = counts.shape[0]
    t = idx.shape[1]
    nb = (2 * t) // tm_e + n_exp
    experts = jnp.arange(n_exp, dtype=jnp.int32)
    padded = ((counts + tm_e - 1) // tm_e) * tm_e
    pends = jnp.cumsum(padded)
    pstarts = pends - padded
    starts = jnp.cumsum(counts) - counts
    pos = jnp.sum(jnp.where(idx[None] == experts[:, None, None], pstarts[:, None, None], 0), axis=0) + rank
    n_used = (pends[-1] // tm_e).astype(jnp.int32).reshape(1)
    first_slot = jnp.arange(nb, dtype=jnp.int32) * tm_e
    block_expert = jnp.minimum(jnp.sum(pends[None, :] <= first_slot[:, None], axis=1), n_exp - 1).astype(jnp.int32)
    shift = jnp.sum(jnp.where(block_expert[:, None] == experts[None, :], (pstarts - starts)[None, :], 0), axis=1)
    block_src = jnp.clip(first_slot - shift, 0, 2 * t - 1).astype(jnp.int32)
    tok = jnp.arange(t, dtype=jnp.int32)
    keys = jnp.sort((idx * t + tok[None, :]).reshape(-1))
    tok_sorted = keys % t
    n_rows = (2 * t) // LANES + tm_e // LANES + 1
    tok_rows = jnp.zeros((n_rows * LANES,), jnp.int32).at[:2 * t].set(tok_sorted).reshape(n_rows, LANES)
    nt = t // tm_c
    pos_rows = pos.reshape(2, nt, tm_c).transpose(1, 0, 2).reshape(nt * 2 * tm_c // LANES, LANES)
    return block_expert, block_src, n_used, tok_rows, pos_rows


def kernel(x, c, w_mod, b_mod, norm1_w, w_in, conv_dw_w, conv_dw_b, conv_ln_w, conv_ln_b,
           ssd_conv_w, ssd_conv_b, a_log, dt_bias, d_skip, ssd_norm_w, w_out, norm2_w, w_router,
           router_bias, w_gate, w_up, w_down, final_norm_w):
    bsz, seq, d = x.shape
    depth = w_mod.shape[0]
    t = bsz * seq
    d_conv = conv_dw_w.shape[2]
    d_ssd = ssd_norm_w.shape[1]
    xbc_w = ssd_conv_w.shape[2]
    heads = a_log.shape[1]
    n_main = w_in.shape[2] - heads
    tm_e = min(TM_EXPERT, seq)
    tm_c = min(TM_COMBINE, seq)
    assert seq % SSD_CHUNK == 0 and n_main % TN_PROJ == 0 and d_conv % CONV_GROUP == 0
    assert 2 * d_conv % d_ssd == 0 and (2 * d_conv + d_ssd) % xbc_w == 0
    assert tm_e % LANES == 0 and tm_c % LANES == 0 and heads <= LANES

    row3 = lambda v: v.reshape(depth, 1, -1).astype(F32)
    lane_pad = lambda v: jnp.zeros((depth, 1, LANES), F32).at[:, 0, :heads].set(v.astype(F32))
    w_in_bf = w_in.astype(BF16)
    w_dt = jnp.zeros((depth, d, LANES), BF16).at[:, :, :heads].set(w_in_bf[:, :, n_main:])
    w_out_bf = w_out.astype(BF16)
    wr_hi, wr_lo = _split2(w_router.T)
    head_of_ch = jnp.arange(d_ssd, dtype=jnp.int32) // SSD_HEAD_DIM
    expand = (jnp.arange(LANES, dtype=jnp.int32)[:, None] == head_of_ch[None, :]).astype(BF16)
    dskip_e = jnp.repeat(d_skip.astype(F32), SSD_HEAD_DIM, axis=1).reshape(depth, 1, d_ssd)

    mod = _modulation(c, w_mod, b_mod).reshape(depth, bsz, 6, 1, d)
    x2d = x.reshape(t, d)
    for l in range(depth):
        proj, dt_raw = _in_proj(l, x2d, mod, row3(norm1_w), w_in_bf, w_dt, n_main, seq)
        y_conv = _conformer_conv(l, proj, conv_dw_w, row3(conv_dw_b), row3(conv_ln_w), row3(conv_ln_b),
                                 bsz, seq)
        y_ssd = _ssd(l, proj, dt_raw, ssd_conv_w, row3(ssd_conv_b), lane_pad(a_log), lane_pad(dt_bias),
                     dskip_e, row3(ssd_norm_w), expand, bsz, seq, 2 * d_conv)
        x1, h2, idx, gwt, rank, cnt = _out_proj(l, y_conv, y_ssd, x2d, mod, row3(norm2_w), w_out_bf,
                                                wr_hi, wr_lo, router_bias, seq)
        block_expert, block_src, n_used, tok_rows, pos_rows = _routing_tables(idx, rank, cnt[:, 0], tm_e, tm_c)
        ys = _experts(l, h2, block_expert, block_src, n_used, tok_rows, w_gate, w_up, w_down, tm_e)
        x2d = _combine(l, pos_rows, ys, x1, gwt, mod, final_norm_w, seq, tm_c, final=(l == depth - 1))
    return x2d.reshape(bsz, seq, d)
```

```python
import functools

import jax
import jax.numpy as jnp
from jax import lax
from jax.experimental import pallas as pl
from jax.experimental.pallas import tpu as pltpu

F32 = jnp.float32
BF16 = jnp.bfloat16
EPS = 1e-6

LANES = 128
LANE_SHIFT = LANES.bit_length() - 1
SUBLANES = 8
VMEM_LIMIT_BYTES = 56 * 1024 * 1024

CONV_GROUP = 128
CONV_HALO = 32
SSD_HEAD_DIM = 64
SSD_GROUPS = 2
SSD_STATE = 128
SSD_CHUNK = 128
SSD_HALO = 8
N_EXPERT_GROUPS = 8
EXPERTS_PER_GROUP = 4

TM_PROJ = 512
TL_CONV = 256
CONV_ROWS = 64
TM_EXPERT = 256
TM_COMBINE = 256
TN_MOD = 1024
TN_PROJ = 512


def _cparams(sem):
    return pltpu.CompilerParams(dimension_semantics=sem, vmem_limit_bytes=VMEM_LIMIT_BYTES)


def _sigmoid(v):
    return jax.nn.sigmoid(v)


def _split2(v):
    hi = v.astype(BF16)
    lo = (v - hi.astype(F32)).astype(BF16)
    return hi, lo


def _split3(v):
    p1 = v.astype(BF16)
    r1 = v - p1.astype(F32)
    p2 = r1.astype(BF16)
    p3 = (r1 - p2.astype(F32)).astype(BF16)
    return p1, p2, p3


def _dot(a, b):
    return jnp.dot(a, b, preferred_element_type=F32)


def _dot_nt(a, b):
    return lax.dot_general(a, b, (((1,), (1,)), ((), ())), preferred_element_type=F32)


def _mod_kernel(c_ref, w_ref, b_ref, o_ref):
    c = c_ref[...]
    nb = c.shape[0]
    ca = c * _sigmoid(c)
    c_hi, c_lo = _split2(ca)
    w_hi, w_lo = _split2(w_ref[...])
    r = _dot(jnp.concatenate([c_hi, c_lo], axis=0), w_hi)
    o_ref[...] = r[:nb] + r[nb:] + _dot(c_hi, w_lo) + b_ref[...]


def _modulation(c, w_mod, b_mod):
    depth, d, n = w_mod.shape
    bsz = c.shape[0]
    tn = min(TN_MOD, n)
    return pl.pallas_call(
        _mod_kernel,
        grid=(depth, n // tn),
        in_specs=[
            pl.BlockSpec((bsz, d), lambda l, j: (0, 0)),
            pl.BlockSpec((None, d, tn), lambda l, j: (l, 0, j)),
            pl.BlockSpec((None, 1, tn), lambda l, j: (l, 0, j)),
        ],
        out_specs=pl.BlockSpec((None, bsz, tn), lambda l, j: (l, 0, j)),
        out_shape=jax.ShapeDtypeStruct((depth, bsz, n), F32),
        compiler_params=_cparams(("arbitrary", "arbitrary")),
        name="modulation",
    )(c, w_mod, b_mod.reshape(depth, 1, n))


def _mod_spec(l, j, per_b, d):
    return pl.BlockSpec((None, None, None, 1, d), lambda i: (l, i // per_b, j, 0, 0))


def _rms_modulate(x, nw, sc, sh):
    ms = jnp.mean(x * x, axis=-1, keepdims=True)
    return (x * lax.rsqrt(ms + EPS) * nw) * (1.0 + sc) + sh


def _inproj_kernel(x_ref, sh_ref, sc_ref, nw_ref, w_ref, wdt_ref, o_ref, odt_ref):
    hb = _rms_modulate(x_ref[...], nw_ref[...], sc_ref[...], sh_ref[...]).astype(BF16)
    n_main = o_ref.shape[1]
    for c0 in range(0, n_main, TN_PROJ):
        o_ref[:, c0:c0 + TN_PROJ] = _dot(hb, w_ref[:, c0:c0 + TN_PROJ]).astype(BF16)
    odt_ref[...] = _dot(hb, wdt_ref[...])


def _in_proj(l, x2d, mod, norm_w, w_in, w_dt, n_main, seq):
    t, d = x2d.shape
    tm = min(TM_PROJ, seq)
    per_b = seq // tm
    return pl.pallas_call(
        _inproj_kernel,
        grid=(t // tm,),
        in_specs=[
            pl.BlockSpec((tm, d), lambda i: (i, 0)),
            _mod_spec(l, 0, per_b, d),
            _mod_spec(l, 1, per_b, d),
            pl.BlockSpec((None, 1, d), lambda i: (l, 0, 0)),
            pl.BlockSpec((None, d, n_main), lambda i: (l, 0, 0), pipeline_mode=pl.Buffered(1)),
            pl.BlockSpec((None, d, LANES), lambda i: (l, 0, 0), pipeline_mode=pl.Buffered(1)),
        ],
        out_specs=[
            pl.BlockSpec((tm, n_main), lambda i: (i, 0)),
            pl.BlockSpec((tm, LANES), lambda i: (i, 0)),
        ],
        out_shape=[
            jax.ShapeDtypeStruct((t, n_main), BF16),
            jax.ShapeDtypeStruct((t, LANES), F32),
        ],
        compiler_params=_cparams(("arbitrary",)),
        name="in_proj",
    )(x2d, mod, mod, norm_w, w_in, w_dt)


def _conv_kernel(a_ref, g_ref, w_ref, b_ref, lnw_ref, lnb_ref, o_ref, ubuf):
    tl, ch = o_ref.shape
    taps = w_ref.shape[0]
    n = CONV_HALO + tl
    i = pl.program_id(1)

    @pl.when(i == 0)
    def _():
        ubuf[0, 0:CONV_HALO, :] = jnp.zeros((CONV_HALO, ch), F32)

    @pl.when(i > 0)
    def _():
        ubuf[0, 0:CONV_HALO, :] = ubuf[0, tl:tl + CONV_HALO, :]

    ubuf[0, CONV_HALO:n, :] = a_ref[...].astype(F32) * _sigmoid(g_ref[...].astype(F32))
    for s in range(1, SUBLANES):
        ubuf[s, 0:n - s, :] = ubuf[0, s:n, :]

    rows = min(CONV_ROWS, tl)
    for cb in range(ch // CONV_GROUP):
        cs = slice(cb * CONV_GROUP, (cb + 1) * CONV_GROUP)
        bias = b_ref[:, cs]
        lnw = lnw_ref[:, cs]
        lnb = lnb_ref[:, cs]
        for rs in range(tl // rows):
            r0 = CONV_HALO - (taps - 1) + rs * rows
            acc = jnp.zeros((rows, CONV_GROUP), F32)
            for k in range(taps):
                phase = (r0 + k) % SUBLANES
                a0 = r0 + k - phase
                acc = acc + w_ref[k:k + 1, cs] * ubuf[phase, a0:a0 + rows, cs]
            acc = acc + bias
            mu = jnp.mean(acc, axis=-1, keepdims=True)
            dev = acc - mu
            var = jnp.mean(dev * dev, axis=-1, keepdims=True)
            yn = dev * lax.rsqrt(var + EPS) * lnw + lnb
            o_ref[rs * rows:(rs + 1) * rows, cs] = (yn * _sigmoid(yn)).astype(o_ref.dtype)


def _conformer_conv(l, proj, dw_w, dw_b, ln_w, ln_b, bsz, seq):
    _, taps, ch = dw_w.shape
    tl = min(TL_CONV, seq)
    nl = seq // tl
    row = lambda b, i: b * nl + i
    vec = pl.BlockSpec((None, 1, ch), lambda b, i: (l, 0, 0))
    return pl.pallas_call(
        _conv_kernel,
        grid=(bsz, nl),
        in_specs=[
            pl.BlockSpec((tl, ch), lambda b, i: (row(b, i), 0)),
            pl.BlockSpec((tl, ch), lambda b, i: (row(b, i), 1)),
            pl.BlockSpec((None, taps, ch), lambda b, i: (l, 0, 0)),
            vec, vec, vec,
        ],
        out_specs=pl.BlockSpec((tl, ch), lambda b, i: (row(b, i), 0)),
        out_shape=jax.ShapeDtypeStruct((bsz * seq, ch), BF16),
        scratch_shapes=[pltpu.VMEM((SUBLANES, CONV_HALO + tl, ch), F32)],
        compiler_params=_cparams(("arbitrary", "arbitrary")),
        name="conformer_conv",
    )(proj, proj, dw_w, dw_b, ln_w, ln_b)


def _ssd_kernel(z_ref, xbc_ref, dt_ref, cw_ref, cb_ref, alog_ref, dtb_ref, dskip_ref, nw_ref,
                expand_ref, o_ref, xbuf, state):
    q, d_ssd = o_ref.shape
    gw = d_ssd // SSD_GROUPS
    gn = SSD_GROUPS * SSD_STATE
    taps = cw_ref.shape[0]
    c = pl.program_id(1)

    @pl.when(c == 0)
    def _():
        xbuf[0:SSD_HALO, :] = jnp.zeros((SSD_HALO, xbuf.shape[1]), F32)
        state[...] = jnp.zeros(state.shape, F32)

    @pl.when(c > 0)
    def _():
        xbuf[0:SSD_HALO, :] = xbuf[q:q + SSD_HALO, :]

    xbuf[SSD_HALO:SSD_HALO + q, :] = xbc_ref[...].astype(F32)
    acc = jnp.zeros((q, xbuf.shape[1]), F32) + cb_ref[...]
    for k in range(taps):
        r0 = SSD_HALO - (taps - 1) + k
        acc = acc + cw_ref[k:k + 1, :] * xbuf[r0:r0 + q, :]
    xc = acc * _sigmoid(acc)
    xs = xc[:, :d_ssd]
    bm = xc[:, d_ssd:d_ssd + gn]
    cm = xc[:, d_ssd + gn:]

    dt = jax.nn.softplus(dt_ref[...] + dtb_ref[...])
    a = -jnp.exp(alog_ref[...])
    da = dt * a
    rows = lax.broadcasted_iota(jnp.int32, (q, q), 0)
    cols = lax.broadcasted_iota(jnp.int32, (q, q), 1)
    tril = rows >= cols
    tri = jnp.where(tril, 1.0, 0.0).astype(BF16)
    d1, d2, d3 = _split3(da)
    cs = _dot(tri, d1) + _dot(tri, d2) + _dot(tri, d3)
    cs_t = cs.T
    cs_last = cs[q - 1:q, :]

    expand = expand_ref[...]

    def per_channel(v):
        hi, lo = _split2(v)
        return _dot(hi, expand) + _dot(lo, expand)

    dt_e = per_channel(dt)
    ecs_e = per_channel(jnp.exp(cs))
    wds_e = per_channel(dt * jnp.exp(cs_last - cs))
    xd = (xs * dt_e).astype(BF16)
    xds = (xs * wds_e).astype(BF16)
    chunk_decay = ecs_e[q - 1:q, :]

    lane = lax.broadcasted_iota(jnp.int32, (q, LANES), 1)
    heads_per_group = gw // SSD_HEAD_DIM
    y_parts = []
    for g in range(SSD_GROUPS):
        bg = bm[:, g * SSD_STATE:(g + 1) * SSD_STATE]
        cg = cm[:, g * SSD_STATE:(g + 1) * SSD_STATE].astype(BF16)
        cb_mat = _dot_nt(cg, bg.astype(BF16))
        st = state[g]
        y_off = _dot(cg, st.astype(BF16)) * ecs_e[:, g * gw:(g + 1) * gw]
        diag_parts = []
        for pr in range(heads_per_group // 2):
            lo_col = g * gw + pr * LANES
            xd_blk = xd[:, lo_col:lo_col + LANES]
            yp = jnp.zeros((q, LANES), F32)
            for half in range(2):
                h = g * heads_per_group + 2 * pr + half
                diff = cs[:, h:h + 1] - cs_t[h:h + 1, :]
                decay = jnp.exp(jnp.where(tril, diff, -jnp.inf))
                m = (cb_mat * decay).astype(BF16)
                keep = (lane >= SSD_HEAD_DIM) if half else (lane < SSD_HEAD_DIM)
                yp = yp + _dot(m, jnp.where(keep, xd_blk, jnp.zeros_like(xd_blk)))
            diag_parts.append(yp)
        y_parts.append(jnp.concatenate(diag_parts, axis=1) + y_off)
        new = _dot(bg.T.astype(BF16), xds[:, g * gw:(g + 1) * gw])
        state[g] = st * chunk_decay[:, g * gw:(g + 1) * gw] + new

    y = jnp.concatenate(y_parts, axis=1) + dskip_ref[...] * xs
    z = z_ref[...].astype(F32)
    y = y * (z * _sigmoid(z))
    outs = []
    for g in range(SSD_GROUPS):
        yg = y[:, g * gw:(g + 1) * gw]
        outs.append(yg * lax.rsqrt(jnp.mean(yg * yg, axis=-1, keepdims=True) + EPS))
    o_ref[...] = (jnp.concatenate(outs, axis=1) * nw_ref[...]).astype(o_ref.dtype)


def _ssd(l, proj, dt_raw, conv_w, conv_b, a_log, dt_bias, dskip_e, norm_w, expand, bsz, seq, z_col0):
    _, taps, xbc_w = conv_w.shape
    d_ssd = norm_w.shape[2]
    q = SSD_CHUNK
    nc = seq // q
    gw = d_ssd // SSD_GROUPS
    row = lambda b, c: b * nc + c
    vec = lambda n: pl.BlockSpec((None, 1, n), lambda b, c: (l, 0, 0))
    return pl.pallas_call(
        _ssd_kernel,
        grid=(bsz, nc),
        in_specs=[
            pl.BlockSpec((q, d_ssd), lambda b, c: (row(b, c), z_col0 // d_ssd)),
            pl.BlockSpec((q, xbc_w), lambda b, c: (row(b, c), (z_col0 + d_ssd) // xbc_w)),
            pl.BlockSpec((q, LANES), lambda b, c: (row(b, c), 0)),
            pl.BlockSpec((None, taps, xbc_w), lambda b, c: (l, 0, 0)),
            vec(xbc_w), vec(LANES), vec(LANES), vec(d_ssd), vec(d_ssd),
            pl.BlockSpec((LANES, d_ssd), lambda b, c: (0, 0)),
        ],
        out_specs=pl.BlockSpec((q, d_ssd), lambda b, c: (row(b, c), 0)),
        out_shape=jax.ShapeDtypeStruct((bsz * seq, d_ssd), BF16),
        scratch_shapes=[
            pltpu.VMEM((SSD_HALO + q, xbc_w), F32),
            pltpu.VMEM((SSD_GROUPS, SSD_STATE, gw), F32),
        ],
        compiler_params=_cparams(("arbitrary", "arbitrary")),
        name="ssd",
    )(proj, proj, dt_raw, conv_w, conv_b, a_log, dt_bias, dskip_e, norm_w, expand)


def _outproj_kernel(yc_ref, ys_ref, x_ref, g1_ref, sh2_ref, sc2_ref, nw_ref, w_ref, wr_hi_ref,
                    wr_lo_ref, rb_ref, x1_ref, h2_ref, idx_ref, gwt_ref, rank_ref, cnt_ref, base):
    tm = x_ref.shape[0]
    n_exp = wr_hi_ref.shape[0]
    dc = yc_ref.shape[1]
    i = pl.program_id(0)

    @pl.when(i == 0)
    def _():
        base[...] = jnp.zeros(base.shape, F32)

    mix = _dot(yc_ref[...], w_ref[0:dc, :]) + _dot(ys_ref[...], w_ref[dc:, :])
    x1 = x_ref[...] + g1_ref[...] * mix
    x1_ref[...] = x1
    h2 = _rms_modulate(x1, nw_ref[...], sc2_ref[...], sh2_ref[...])
    h2_ref[...] = h2

    h_hi, h_lo = _split2(h2)
    wr_hi = wr_hi_ref[...]
    logits = _dot_nt(wr_hi, h_hi) + _dot_nt(wr_lo_ref[...], h_hi) + _dot_nt(wr_hi, h_lo)
    s = _sigmoid(logits)
    s_sel = s + rb_ref[...]
    eiota = lax.broadcasted_iota(jnp.int32, s.shape, 0)
    group_of = lax.shift_right_logical(eiota, EXPERTS_PER_GROUP.bit_length() - 1)
    in_group = jnp.bitwise_and(eiota, EXPERTS_PER_GROUP - 1)
    pair_max = jnp.full(s.shape, -jnp.inf, F32)
    for j in range(1, EXPERTS_PER_GROUP):
        partner = pltpu.roll(s_sel, n_exp - j, 0)
        pair_max = jnp.maximum(pair_max, jnp.where(in_group < EXPERTS_PER_GROUP - j,
                                                   s_sel + partner, -jnp.inf))
    gmax = jnp.max(pair_max, axis=0, keepdims=True)
    best = jnp.min(jnp.where(pair_max == gmax, group_of, N_EXPERT_GROUPS), axis=0, keepdims=True)
    masked = jnp.where(group_of == best, s_sel, -jnp.inf)
    m1 = jnp.max(masked, axis=0, keepdims=True)
    i1 = jnp.min(jnp.where(masked == m1, eiota, n_exp), axis=0, keepdims=True)
    masked = jnp.where(eiota == i1, -jnp.inf, masked)
    m2 = jnp.max(masked, axis=0, keepdims=True)
    i2 = jnp.min(jnp.where(masked == m2, eiota, n_exp), axis=0, keepdims=True)
    oh1 = eiota == i1
    oh2 = eiota == i2
    a1 = jnp.sum(jnp.where(oh1, s, 0.0), axis=0, keepdims=True)
    a2 = jnp.sum(jnp.where(oh2, s, 0.0), axis=0, keepdims=True)
    tot = a1 + a2
    idx_ref[0:1, :] = i1
    idx_ref[1:2, :] = i2
    liota = lax.broadcasted_iota(jnp.int32, (LANES, tm), 0)
    gates = jnp.where(liota == 0, a1 / tot, jnp.where(liota == 1, a2 / tot, 0.0))
    gwt_ref[...] = gates.T

    oh_any = jnp.where(oh1 | oh2, 1.0, 0.0)
    r_i = lax.broadcasted_iota(jnp.int32, (tm, tm), 0)
    c_i = lax.broadcasted_iota(jnp.int32, (tm, tm), 1)
    upper = jnp.where(r_i <= c_i, 1.0, 0.0).astype(BF16)
    incl = _dot(oh_any.astype(BF16), upper)
    rank_all = base[...] + incl - oh_any
    rank_ref[0:1, :] = jnp.sum(jnp.where(oh1, rank_all, 0.0), axis=0, keepdims=True).astype(jnp.int32)
    rank_ref[1:2, :] = jnp.sum(jnp.where(oh2, rank_all, 0.0), axis=0, keepdims=True).astype(jnp.int32)
    new_base = base[...] + jnp.sum(oh_any, axis=1, keepdims=True)
    base[...] = new_base
    cnt_ref[...] = jnp.broadcast_to(new_base, cnt_ref.shape).astype(jnp.int32)


def _out_proj(l, y_conv, y_ssd, x2d, mod, norm_w, w_out, wr_hi, wr_lo, router_bias, seq):
    t, d = x2d.shape
    dc = y_conv.shape[1]
    ds = y_ssd.shape[1]
    n_exp = wr_hi.shape[0]
    tm = min(TM_PROJ, seq)
    per_b = seq // tm
    const = lambda shape: pl.BlockSpec(shape, lambda i: (0, 0))
    return pl.pallas_call(
        _outproj_kernel,
        grid=(t // tm,),
        in_specs=[
            pl.BlockSpec((tm, dc), lambda i: (i, 0)),
            pl.BlockSpec((tm, ds), lambda i: (i, 0)),
            pl.BlockSpec((tm, d), lambda i: (i, 0)),
            _mod_spec(l, 2, per_b, d), _mod_spec(l, 3, per_b, d), _mod_spec(l, 4, per_b, d),
            pl.BlockSpec((None, 1, d), lambda i: (l, 0, 0)),
            pl.BlockSpec((None, dc + ds, d), lambda i: (l, 0, 0), pipeline_mode=pl.Buffered(1)),
            const((n_exp, d)), const((n_exp, d)), const((n_exp, 1)),
        ],
        out_specs=[
            pl.BlockSpec((tm, d), lambda i: (i, 0)),
            pl.BlockSpec((tm, d), lambda i: (i, 0)),
            pl.BlockSpec((2, tm), lambda i: (0, i)),
            pl.BlockSpec((tm, LANES), lambda i: (i, 0)),
            pl.BlockSpec((2, tm), lambda i: (0, i)),
            const((n_exp, LANES)),
        ],
        out_shape=[
            jax.ShapeDtypeStruct((t, d), F32),
            jax.ShapeDtypeStruct((t, d), F32),
            jax.ShapeDtypeStruct((2, t), jnp.int32),
            jax.ShapeDtypeStruct((t, LANES), F32),
            jax.ShapeDtypeStruct((2, t), jnp.int32),
            jax.ShapeDtypeStruct((n_exp, LANES), jnp.int32),
        ],
        scratch_shapes=[pltpu.VMEM((n_exp, 1), F32)],
        compiler_params=_cparams(("arbitrary",)),
        name="out_proj_router",
    )(y_conv, y_ssd, x2d, mod, mod, mod, norm_w, w_out, wr_hi, wr_lo,
      router_bias.reshape(n_exp, 1).astype(F32))


def _gather_rows(idx_hbm, row0, delta, src_hbm, idx_smem, isem, buf, sem):
    copies = [pltpu.make_async_copy(idx_hbm.at[row0 + i], idx_smem.at[pl.ds(i * LANES, LANES)], isem)
              for i in range(idx_smem.shape[0] // LANES)]
    for cp in copies:
        cp.start()
    for cp in copies:
        cp.wait()
    for r in range(buf.shape[0]):
        pltpu.make_async_copy(src_hbm.at[pl.ds(idx_smem[delta + r], 1)], buf.at[pl.ds(r, 1)], sem).start()


def _wait_rows(src_hbm, buf, sem):
    pltpu.make_async_copy(src_hbm.at[pl.ds(0, buf.shape[0])], buf, sem).wait()


def _expert_kernel(be_ref, src_ref, nused_ref, tok_hbm, h_hbm, wg_ref, wu_ref, wd_ref, o_ref,
                   idx0, idx1, xbuf0, xbuf1, sem, isem, wg_bf, wu_bf, wd_bf):
    s = pl.program_id(0)
    b = s - 1
    n_used = nused_ref[0]
    idx = (idx0, idx1)
    xbuf = (xbuf0, xbuf1)

    def gather(blk, p):
        src = src_ref[blk]
        _gather_rows(tok_hbm, lax.shift_right_logical(src, LANE_SHIFT), jnp.bitwise_and(src, LANES - 1),
                     h_hbm, idx[p], isem, xbuf[p], sem.at[p])

    def compute(p):
        x = xbuf[p][...].astype(BF16)
        gate = _dot(x, wg_bf[...])
        up = _dot(x, wu_bf[...])
        hb = (gate * _sigmoid(gate) * up).astype(BF16)
        o_ref[...] = _dot(hb, wd_bf[...])

    @pl.when(s == 0)
    def _():
        gather(0, 0)

    @pl.when((b >= 0) & (b < n_used) & ((b == 0) | (be_ref[jnp.maximum(b, 0)] != be_ref[jnp.maximum(b - 1, 0)])))
    def _():
        wg_bf[...] = wg_ref[...].astype(BF16)
        wu_bf[...] = wu_ref[...].astype(BF16)
        wd_bf[...] = wd_ref[...].astype(BF16)

    for p in range(2):
        @pl.when((b >= 0) & (b < n_used) & (lax.rem(b, 2) == p))
        def _():
            _wait_rows(h_hbm, xbuf[p], sem.at[p])
            gather(jnp.minimum(s, n_used - 1), 1 - p)
            compute(p)

        @pl.when((b == n_used) & (lax.rem(b, 2) == p))
        def _():
            _wait_rows(h_hbm, xbuf[p], sem.at[p])

    @pl.when(b >= n_used)
    def _():
        o_ref[...] = jnp.zeros(o_ref.shape, o_ref.dtype)


def _experts(l, h2, block_expert, block_src, n_used, tok_rows, w_gate, w_up, w_down, tm):
    t, d = h2.shape
    nb = block_expert.shape[0]
    _, n_exp, _, de = w_gate.shape
    blk = lambda s: jnp.maximum(s - 1, 0)
    wspec = lambda a, b_: pl.BlockSpec((None, None, a, b_), lambda s, be, src, nu: (l, be[blk(s)], 0, 0))
    grid_spec = pltpu.PrefetchScalarGridSpec(
        num_scalar_prefetch=3,
        grid=(nb + 1,),
        in_specs=[
            pl.BlockSpec(memory_space=pl.ANY),
            pl.BlockSpec(memory_space=pl.ANY),
            wspec(d, de), wspec(d, de), wspec(de, d),
        ],
        out_specs=pl.BlockSpec((tm, d), lambda s, be, src, nu: (blk(s), 0)),
        scratch_shapes=[
            pltpu.SMEM((tm + LANES,), jnp.int32),
            pltpu.SMEM((tm + LANES,), jnp.int32),
            pltpu.VMEM((tm, d), F32),
            pltpu.VMEM((tm, d), F32),
            pltpu.SemaphoreType.DMA((2,)),
            pltpu.SemaphoreType.DMA(()),
            pltpu.VMEM((d, de), BF16),
            pltpu.VMEM((d, de), BF16),
            pltpu.VMEM((de, d), BF16),
        ],
    )
    return pl.pallas_call(
        _expert_kernel,
        grid_spec=grid_spec,
        out_shape=jax.ShapeDtypeStruct((nb * tm, d), F32),
        compiler_params=_cparams(("arbitrary",)),
        name="experts",
    )(block_expert, block_src, n_used, tok_rows, h2, w_gate, w_up, w_down)


def _combine_kernel(final, pos_hbm, ys_hbm, x1_ref, gwt_ref, g2_ref, fnw_ref, o_ref,
                    idx0, idx1, ybuf0, ybuf1, sem, isem):
    s = pl.program_id(0)
    nt = pl.num_programs(0) - 1
    b = s - 1
    tm = x1_ref.shape[0]
    idx = (idx0, idx1)
    ybuf = (ybuf0, ybuf1)
    rows_per_tile = 2 * tm // LANES

    def gather(tile, p):
        _gather_rows(pos_hbm, tile * rows_per_tile, 0, ys_hbm, idx[p], isem, ybuf[p], sem.at[p])

    def finish(p):
        gates = gwt_ref[...]
        y = gates[:, 0:1] * ybuf[p][0:tm, :] + gates[:, 1:2] * ybuf[p][tm:2 * tm, :]
        x2 = x1_ref[...] + g2_ref[...] * y
        if final:
            ms = jnp.mean(x2 * x2, axis=-1, keepdims=True)
            x2 = x2 * lax.rsqrt(ms + EPS) * fnw_ref[...]
        o_ref[...] = x2

    @pl.when(s == 0)
    def _():
        gather(0, 0)

    for p in range(2):
        @pl.when((b >= 0) & (s < nt) & (lax.rem(b, 2) == p))
        def _():
            _wait_rows(ys_hbm, ybuf[p], sem.at[p])
            gather(s, 1 - p)
            finish(p)

        @pl.when((s == nt) & (lax.rem(b, 2) == p))
        def _():
            _wait_rows(ys_hbm, ybuf[p], sem.at[p])
            finish(p)


def _combine(l, pos_rows, ys, x1, gwt, mod, final_norm_w, seq, tm, final):
    t, d = x1.shape
    nt = t // tm
    per_b = seq // tm
    tile = lambda s: jnp.maximum(s - 1, 0)
    return pl.pallas_call(
        functools.partial(_combine_kernel, final),
        grid=(nt + 1,),
        in_specs=[
            pl.BlockSpec(memory_space=pl.ANY),
            pl.BlockSpec(memory_space=pl.ANY),
            pl.BlockSpec((tm, d), lambda s: (tile(s), 0)),
            pl.BlockSpec((tm, LANES), lambda s: (tile(s), 0)),
            pl.BlockSpec((None, None, None, 1, d), lambda s: (l, tile(s) // per_b, 5, 0, 0)),
            pl.BlockSpec((1, d), lambda s: (0, 0)),
        ],
        out_specs=pl.BlockSpec((tm, d), lambda s: (tile(s), 0)),
        out_shape=jax.ShapeDtypeStruct((t, d), F32),
        scratch_shapes=[
            pltpu.SMEM((2 * tm,), jnp.int32),
            pltpu.SMEM((2 * tm,), jnp.int32),
            pltpu.VMEM((2 * tm, d), F32),
            pltpu.VMEM((2 * tm, d), F32),
            pltpu.SemaphoreType.DMA((2,)),
            pltpu.SemaphoreType.DMA(()),
        ],
        compiler_params=_cparams(("arbitrary",)),
        name="combine",
    )(pos_rows, ys, x1, gwt, mod, final_norm_w.reshape(1, d).astype(F32))


def _routing_tables(idx, rank, counts, tm_e, tm_c):
    n_exp = counts.shape[0]
    t = idx.shape[1]
    nb = (2 * t) // tm_e + n_exp
    experts = jnp.arange(n_exp, dtype=jnp.int32)
    padded = ((counts + tm_e - 1) // tm_e) * tm_e
    pends = jnp.cumsum(padded)
    pstarts = pends - padded
    starts = jnp.cumsum(counts) - counts
    pos = jnp.sum(jnp.where(idx[None] == experts[:, None, None], pstarts[:, None, None], 0), axis=0) + rank
    n_used = (pends[-1] // tm_e).astype(jnp.int32).reshape(1)
    first_slot = jnp.arange(nb, dtype=jnp.int32) * tm_e
    block_expert = jnp.minimum(jnp.sum(pends[None, :] <= first_slot[:, None], axis=1), n_exp - 1).astype(jnp.int32)
    shift = jnp.sum(jnp.where(block_expert[:, None] == experts[None, :], (pstarts - starts)[None, :], 0), axis=1)
    block_src = jnp.clip(first_slot - shift, 0, 2 * t - 1).astype(jnp.int32)
    tok = jnp.arange(t, dtype=jnp.int32)
    keys = jnp.sort((idx * t + tok[None, :]).reshape(-1))
    tok_sorted = keys % t
    n_rows = (2 * t) // LANES + tm_e // LANES + 1
    tok_rows = jnp.zeros((n_rows * LANES,), jnp.int32).at[:2 * t].set(tok_sorted).reshape(n_rows, LANES)
    nt = t // tm_c
    pos_rows = pos.reshape(2, nt, tm_c).transpose(1, 0, 2).reshape(nt * 2 * tm_c // LANES, LANES)
    return block_expert, block_src, n_used, tok_rows, pos_rows


def kernel(x, c, w_mod, b_mod, norm1_w, w_in, conv_dw_w, conv_dw_b, conv_ln_w, conv_ln_b,
           ssd_conv_w, ssd_conv_b, a_log, dt_bias, d_skip, ssd_norm_w, w_out, norm2_w, w_router,
           router_bias, w_gate, w_up, w_down, final_norm_w):
    bsz, seq, d = x.shape
    depth = w_mod.shape[0]
    t = bsz * seq
    d_conv = conv_dw_w.shape[2]
    d_ssd = ssd_norm_w.shape[1]
    xbc_w = ssd_conv_w.shape[2]
    heads = a_log.shape[1]
    n_main = w_in.shape[2] - heads
    tm_e = min(TM_EXPERT, seq)
    tm_c = min(TM_COMBINE, seq)
    assert seq % SSD_CHUNK == 0 and n_main % TN_PROJ == 0 and d_conv % CONV_GROUP == 0
    assert 2 * d_conv % d_ssd == 0 and (2 * d_conv + d_ssd) % xbc_w == 0
    assert tm_e % LANES == 0 and tm_c % LANES == 0 and heads <= LANES

    row3 = lambda v: v.reshape(depth, 1, -1).astype(F32)
    lane_pad = lambda v: jnp.zeros((depth, 1, LANES), F32).at[:, 0, :heads].set(v.astype(F32))
    w_in_bf = w_in.astype(BF16)
    w_dt = jnp.zeros((depth, d, LANES), BF16).at[:, :, :heads].set(w_in_bf[:, :, n_main:])
    w_out_bf = w_out.astype(BF16)
    wr_hi, wr_lo = _split2(w_router.T)
    head_of_ch = jnp.arange(d_ssd, dtype=jnp.int32) // SSD_HEAD_DIM
    expand = (jnp.arange(LANES, dtype=jnp.int32)[:, None] == head_of_ch[None, :]).astype(BF16)
    dskip_e = jnp.repeat(d_skip.astype(F32), SSD_HEAD_DIM, axis=1).reshape(depth, 1, d_ssd)

    mod = _modulation(c, w_mod, b_mod).reshape(depth, bsz, 6, 1, d)
    x2d = x.reshape(t, d)
    for l in range(depth):
        proj, dt_raw = _in_proj(l, x2d, mod, row3(norm1_w), w_in_bf, w_dt, n_main, seq)
        y_conv = _conformer_conv(l, proj, conv_dw_w, row3(conv_dw_b), row3(conv_ln_w), row3(conv_ln_b),
                                 bsz, seq)
        y_ssd = _ssd(l, proj, dt_raw, ssd_conv_w, row3(ssd_conv_b), lane_pad(a_log), lane_pad(dt_bias),
                     dskip_e, row3(ssd_norm_w), expand, bsz, seq, 2 * d_conv)
        x1, h2, idx, gwt, rank, cnt = _out_proj(l, y_conv, y_ssd, x2d, mod, row3(norm2_w), w_out_bf,
                                                wr_hi, wr_lo, router_bias, seq)
        block_expert, block_src, n_used, tok_rows, pos_rows = _routing_tables(idx, rank, cnt[:, 0], tm_e, tm_c)
        ys = _experts(l, h2, block_expert, block_src, n_used, tok_rows, w_gate, w_up, w_down, tm_e)
        x2d = _combine(l, pos_rows, ys, x1, gwt, mod, final_norm_w, seq, tm_c, final=(l == depth - 1))
    return x2d.reshape(bsz, seq, d)
```

```python
import functools

import jax
import jax.numpy as jnp
from jax import lax
from jax.experimental import pallas as pl
from jax.experimental.pallas import tpu as pltpu

F32 = jnp.float32
BF16 = jnp.bfloat16
EPS = 1e-6

LANES = 128
LANE_SHIFT = LANES.bit_length() - 1
SUBLANES = 8
VMEM_LIMIT_BYTES = 56 * 1024 * 1024

CONV_GROUP = 128
CONV_HALO = 32
SSD_HEAD_DIM = 64
SSD_GROUPS = 2
SSD_STATE = 128
SSD_CHUNK = 128
SSD_HALO = 8
N_EXPERT_GROUPS = 8
EXPERTS_PER_GROUP = 4
PAIRS = ((0, 1), (0, 2), (0, 3), (1, 2), (1, 3), (2, 3))
PAIRS_PER_GROUP = len(PAIRS)
N_BUCKETS = N_EXPERT_GROUPS * PAIRS_PER_GROUP
BUCKET_ROWS = 64

TM_PROJ = 512
TL_CONV = 256
CONV_ROWS = 64
TM_EXPERT = 256
TM_COMBINE = 256
TN_MOD = 1024
TN_PROJ = 512


def _cparams(sem):
    return pltpu.CompilerParams(dimension_semantics=sem, vmem_limit_bytes=VMEM_LIMIT_BYTES)


def _sigmoid(v):
    return jax.nn.sigmoid(v)


def _split2(v):
    hi = v.astype(BF16)
    lo = (v - hi.astype(F32)).astype(BF16)
    return hi, lo


def _split3(v):
    p1 = v.astype(BF16)
    r1 = v - p1.astype(F32)
    p2 = r1.astype(BF16)
    p3 = (r1 - p2.astype(F32)).astype(BF16)
    return p1, p2, p3


def _dot(a, b):
    return jnp.dot(a, b, preferred_element_type=F32)


def _dot_nt(a, b):
    return lax.dot_general(a, b, (((1,), (1,)), ((), ())), preferred_element_type=F32)


def _mod_kernel(c_ref, w_ref, b_ref, o_ref):
    c = c_ref[...]
    nb = c.shape[0]
    ca = c * _sigmoid(c)
    c_hi, c_lo = _split2(ca)
    w_hi, w_lo = _split2(w_ref[...])
    r = _dot(jnp.concatenate([c_hi, c_lo], axis=0), w_hi)
    o_ref[...] = r[:nb] + r[nb:] + _dot(c_hi, w_lo) + b_ref[...]


def _modulation(c, w_mod, b_mod):
    depth, d, n = w_mod.shape
    bsz = c.shape[0]
    tn = min(TN_MOD, n)
    return pl.pallas_call(
        _mod_kernel,
        grid=(depth, n // tn),
        in_specs=[
            pl.BlockSpec((bsz, d), lambda l, j: (0, 0)),
            pl.BlockSpec((None, d, tn), lambda l, j: (l, 0, j)),
            pl.BlockSpec((None, 1, tn), lambda l, j: (l, 0, j)),
        ],
        out_specs=pl.BlockSpec((None, bsz, tn), lambda l, j: (l, 0, j)),
        out_shape=jax.ShapeDtypeStruct((depth, bsz, n), F32),
        compiler_params=_cparams(("arbitrary", "arbitrary")),
        name="modulation",
    )(c, w_mod, b_mod.reshape(depth, 1, n))


def _mod_spec(l, j, per_b, d):
    return pl.BlockSpec((None, None, None, 1, d), lambda i: (l, i // per_b, j, 0, 0))


def _rms_modulate(x, nw, sc, sh):
    ms = jnp.mean(x * x, axis=-1, keepdims=True)
    return (x * lax.rsqrt(ms + EPS) * nw) * (1.0 + sc) + sh


def _inproj_kernel(x_ref, sh_ref, sc_ref, nw_ref, w_ref, wdt_ref, o_ref, odt_ref):
    hb = _rms_modulate(x_ref[...], nw_ref[...], sc_ref[...], sh_ref[...]).astype(BF16)
    n_main = o_ref.shape[1]
    for c0 in range(0, n_main, TN_PROJ):
        o_ref[:, c0:c0 + TN_PROJ] = _dot(hb, w_ref[:, c0:c0 + TN_PROJ]).astype(BF16)
    odt_ref[...] = _dot(hb, wdt_ref[...])


def _in_proj(l, x2d, mod, norm_w, w_in, w_dt, n_main, seq):
    t, d = x2d.shape
    tm = min(TM_PROJ, seq)
    per_b = seq // tm
    return pl.pallas_call(
        _inproj_kernel,
        grid=(t // tm,),
        in_specs=[
            pl.BlockSpec((tm, d), lambda i: (i, 0)),
            _mod_spec(l, 0, per_b, d),
            _mod_spec(l, 1, per_b, d),
            pl.BlockSpec((None, 1, d), lambda i: (l, 0, 0)),
            pl.BlockSpec((None, d, n_main), lambda i: (l, 0, 0), pipeline_mode=pl.Buffered(1)),
            pl.BlockSpec((None, d, LANES), lambda i: (l, 0, 0), pipeline_mode=pl.Buffered(1)),
        ],
        out_specs=[
            pl.BlockSpec((tm, n_main), lambda i: (i, 0)),
            pl.BlockSpec((tm, LANES), lambda i: (i, 0)),
        ],
        out_shape=[
            jax.ShapeDtypeStruct((t, n_main), BF16),
            jax.ShapeDtypeStruct((t, LANES), F32),
        ],
        compiler_params=_cparams(("arbitrary",)),
        name="in_proj",
    )(x2d, mod, mod, norm_w, w_in, w_dt)


def _conv_kernel(a_ref, g_ref, w_ref, b_ref, lnw_ref, lnb_ref, o_ref, ubuf):
    tl, ch = o_ref.shape
    taps = w_ref.shape[0]
    n = CONV_HALO + tl
    i = pl.program_id(1)

    @pl.when(i == 0)
    def _():
        ubuf[0, 0:CONV_HALO, :] = jnp.zeros((CONV_HALO, ch), F32)

    @pl.when(i > 0)
    def _():
        ubuf[0, 0:CONV_HALO, :] = ubuf[0, tl:tl + CONV_HALO, :]

    ubuf[0, CONV_HALO:n, :] = a_ref[...].astype(F32) * _sigmoid(g_ref[...].astype(F32))
    for s in range(1, SUBLANES):
        ubuf[s, 0:n - s, :] = ubuf[0, s:n, :]

    rows = min(CONV_ROWS, tl)
    for cb in range(ch // CONV_GROUP):
        cs = slice(cb * CONV_GROUP, (cb + 1) * CONV_GROUP)
        bias = b_ref[:, cs]
        lnw = lnw_ref[:, cs]
        lnb = lnb_ref[:, cs]
        for rs in range(tl // rows):
            r0 = CONV_HALO - (taps - 1) + rs * rows
            acc = jnp.zeros((rows, CONV_GROUP), F32)
            for k in range(taps):
                phase = (r0 + k) % SUBLANES
                a0 = r0 + k - phase
                acc = acc + w_ref[k:k + 1, cs] * ubuf[phase, a0:a0 + rows, cs]
            acc = acc + bias
            mu = jnp.mean(acc, axis=-1, keepdims=True)
            dev = acc - mu
            var = jnp.mean(dev * dev, axis=-1, keepdims=True)
            yn = dev * lax.rsqrt(var + EPS) * lnw + lnb
            o_ref[rs * rows:(rs + 1) * rows, cs] = (yn * _sigmoid(yn)).astype(o_ref.dtype)


def _conformer_conv(l, proj, dw_w, dw_b, ln_w, ln_b, bsz, seq):
    _, taps, ch = dw_w.shape
    tl = min(TL_CONV, seq)
    nl = seq // tl
    row = lambda b, i: b * nl + i
    vec = pl.BlockSpec((None, 1, ch), lambda b, i: (l, 0, 0))
    return pl.pallas_call(
        _conv_kernel,
        grid=(bsz, nl),
        in_specs=[
            pl.BlockSpec((tl, ch), lambda b, i: (row(b, i), 0)),
            pl.BlockSpec((tl, ch), lambda b, i: (row(b, i), 1)),
            pl.BlockSpec((None, taps, ch), lambda b, i: (l, 0, 0)),
            vec, vec, vec,
        ],
        out_specs=pl.BlockSpec((tl, ch), lambda b, i: (row(b, i), 0)),
        out_shape=jax.ShapeDtypeStruct((bsz * seq, ch), BF16),
        scratch_shapes=[pltpu.VMEM((SUBLANES, CONV_HALO + tl, ch), F32)],
        compiler_params=_cparams(("arbitrary", "arbitrary")),
        name="conformer_conv",
    )(proj, proj, dw_w, dw_b, ln_w, ln_b)


def _ssd_kernel(z_ref, xbc_ref, dt_ref, cw_ref, cb_ref, alog_ref, dtb_ref, dskip_ref, nw_ref,
                expand_ref, o_ref, xbuf, state):
    q, d_ssd = o_ref.shape
    gw = d_ssd // SSD_GROUPS
    gn = SSD_GROUPS * SSD_STATE
    taps = cw_ref.shape[0]
    c = pl.program_id(1)

    @pl.when(c == 0)
    def _():
        xbuf[0:SSD_HALO, :] = jnp.zeros((SSD_HALO, xbuf.shape[1]), F32)
        state[...] = jnp.zeros(state.shape, F32)

    @pl.when(c > 0)
    def _():
        xbuf[0:SSD_HALO, :] = xbuf[q:q + SSD_HALO, :]

    xbuf[SSD_HALO:SSD_HALO + q, :] = xbc_ref[...].astype(F32)
    acc = jnp.zeros((q, xbuf.shape[1]), F32) + cb_ref[...]
    for k in range(taps):
        r0 = SSD_HALO - (taps - 1) + k
        acc = acc + cw_ref[k:k + 1, :] * xbuf[r0:r0 + q, :]
    xc = acc * _sigmoid(acc)
    xs = xc[:, :d_ssd]
    bm = xc[:, d_ssd:d_ssd + gn]
    cm = xc[:, d_ssd + gn:]

    dt = jax.nn.softplus(dt_ref[...] + dtb_ref[...])
    a = -jnp.exp(alog_ref[...])
    da = dt * a
    rows = lax.broadcasted_iota(jnp.int32, (q, q), 0)
    cols = lax.broadcasted_iota(jnp.int32, (q, q), 1)
    tril = rows >= cols
    tri = jnp.where(tril, 1.0, 0.0).astype(BF16)
    d1, d2, d3 = _split3(da)
    cs = _dot(tri, d1) + _dot(tri, d2) + _dot(tri, d3)
    cs_t = cs.T
    cs_last = cs[q - 1:q, :]

    expand = expand_ref[...]

    def per_channel(v):
        hi, lo = _split2(v)
        return _dot(hi, expand) + _dot(lo, expand)

    dt_e = per_channel(dt)
    ecs_e = per_channel(jnp.exp(cs))
    wds_e = per_channel(dt * jnp.exp(cs_last - cs))
    xd = (xs * dt_e).astype(BF16)
    xds = (xs * wds_e).astype(BF16)
    chunk_decay = ecs_e[q - 1:q, :]

    lane = lax.broadcasted_iota(jnp.int32, (q, LANES), 1)
    heads_per_group = gw // SSD_HEAD_DIM
    y_parts = []
    for g in range(SSD_GROUPS):
        bg = bm[:, g * SSD_STATE:(g + 1) * SSD_STATE]
        cg = cm[:, g * SSD_STATE:(g + 1) * SSD_STATE].astype(BF16)
        cb_mat = _dot_nt(cg, bg.astype(BF16))
        st = state[g]
        y_off = _dot(cg, st.astype(BF16)) * ecs_e[:, g * gw:(g + 1) * gw]
        diag_parts = []
        for pr in range(heads_per_group // 2):
            lo_col = g * gw + pr * LANES
            xd_blk = xd[:, lo_col:lo_col + LANES]
            yp = jnp.zeros((q, LANES), F32)
            for half in range(2):
                h = g * heads_per_group + 2 * pr + half
                diff = cs[:, h:h + 1] - cs_t[h:h + 1, :]
                decay = jnp.exp(jnp.where(tril, diff, -jnp.inf))
                m = (cb_mat * decay).astype(BF16)
                keep = (lane >= SSD_HEAD_DIM) if half else (lane < SSD_HEAD_DIM)
                yp = yp + _dot(m, jnp.where(keep, xd_blk, jnp.zeros_like(xd_blk)))
            diag_parts.append(yp)
        y_parts.append(jnp.concatenate(diag_parts, axis=1) + y_off)
        new = _dot(bg.T.astype(BF16), xds[:, g * gw:(g + 1) * gw])
        state[g] = st * chunk_decay[:, g * gw:(g + 1) * gw] + new

    y = jnp.concatenate(y_parts, axis=1) + dskip_ref[...] * xs
    z = z_ref[...].astype(F32)
    y = y * (z * _sigmoid(z))
    outs = []
    for g in range(SSD_GROUPS):
        yg = y[:, g * gw:(g + 1) * gw]
        outs.append(yg * lax.rsqrt(jnp.mean(yg * yg, axis=-1, keepdims=True) + EPS))
    o_ref[...] = (jnp.concatenate(outs, axis=1) * nw_ref[...]).astype(o_ref.dtype)


def _ssd(l, proj, dt_raw, conv_w, conv_b, a_log, dt_bias, dskip_e, norm_w, expand, bsz, seq, z_col0):
    _, taps, xbc_w = conv_w.shape
    d_ssd = norm_w.shape[2]
    q = SSD_CHUNK
    nc = seq // q
    gw = d_ssd // SSD_GROUPS
    row = lambda b, c: b * nc + c
    vec = lambda n: pl.BlockSpec((None, 1, n), lambda b, c: (l, 0, 0))
    return pl.pallas_call(
        _ssd_kernel,
        grid=(bsz, nc),
        in_specs=[
            pl.BlockSpec((q, d_ssd), lambda b, c: (row(b, c), z_col0 // d_ssd)),
            pl.BlockSpec((q, xbc_w), lambda b, c: (row(b, c), (z_col0 + d_ssd) // xbc_w)),
            pl.BlockSpec((q, LANES), lambda b, c: (row(b, c), 0)),
            pl.BlockSpec((None, taps, xbc_w), lambda b, c: (l, 0, 0)),
            vec(xbc_w), vec(LANES), vec(LANES), vec(d_ssd), vec(d_ssd),
            pl.BlockSpec((LANES, d_ssd), lambda b, c: (0, 0)),
        ],
        out_specs=pl.BlockSpec((q, d_ssd), lambda b, c: (row(b, c), 0)),
        out_shape=jax.ShapeDtypeStruct((bsz * seq, d_ssd), BF16),
        scratch_shapes=[
            pltpu.VMEM((SSD_HALO + q, xbc_w), F32),
            pltpu.VMEM((SSD_GROUPS, SSD_STATE, gw), F32),
        ],
        compiler_params=_cparams(("arbitrary", "arbitrary")),
        name="ssd",
    )(proj, proj, dt_raw, conv_w, conv_b, a_log, dt_bias, dskip_e, norm_w, expand)


def _outproj_kernel(yc_ref, ys_ref, x_ref, g1_ref, sh2_ref, sc2_ref, nw_ref, w_ref, wr_hi_ref,
                    wr_lo_ref, rb_ref, x1_ref, h2_ref, qr_ref, cnt_ref, base):
    tm, d = x_ref.shape
    n_exp = wr_hi_ref.shape[0]
    dc = yc_ref.shape[1]
    i = pl.program_id(0)

    @pl.when(i == 0)
    def _():
        base[...] = jnp.zeros(base.shape, F32)

    mix = _dot(yc_ref[...], w_ref[0:dc, :]) + _dot(ys_ref[...], w_ref[dc:, :])
    x1 = x_ref[...] + g1_ref[...] * mix
    x1_ref[...] = x1
    h2 = _rms_modulate(x1, nw_ref[...], sc2_ref[...], sh2_ref[...])
    h2_ref[:, 0:d] = h2

    h_hi, h_lo = _split2(h2)
    wr_hi = wr_hi_ref[...]
    logits = _dot_nt(wr_hi, h_hi) + _dot_nt(wr_lo_ref[...], h_hi) + _dot_nt(wr_hi, h_lo)
    s = _sigmoid(logits)
    s_sel = s + rb_ref[...]
    eiota = lax.broadcasted_iota(jnp.int32, s.shape, 0)
    group_of = lax.shift_right_logical(eiota, EXPERTS_PER_GROUP.bit_length() - 1)
    in_group = jnp.bitwise_and(eiota, EXPERTS_PER_GROUP - 1)
    pair_max = jnp.full(s.shape, -jnp.inf, F32)
    for j in range(1, EXPERTS_PER_GROUP):
        partner = pltpu.roll(s_sel, n_exp - j, 0)
        pair_max = jnp.maximum(pair_max, jnp.where(in_group < EXPERTS_PER_GROUP - j,
                                                   s_sel + partner, -jnp.inf))
    gmax = jnp.max(pair_max, axis=0, keepdims=True)
    best = jnp.min(jnp.where(pair_max == gmax, group_of, N_EXPERT_GROUPS), axis=0, keepdims=True)
    masked = jnp.where(group_of == best, s_sel, -jnp.inf)
    m1 = jnp.max(masked, axis=0, keepdims=True)
    i1 = jnp.min(jnp.where(masked == m1, eiota, n_exp), axis=0, keepdims=True)
    masked = jnp.where(eiota == i1, -jnp.inf, masked)
    m2 = jnp.max(masked, axis=0, keepdims=True)
    i2 = jnp.min(jnp.where(masked == m2, eiota, n_exp), axis=0, keepdims=True)
    e_lo = jnp.minimum(i1, i2)
    e_hi = jnp.maximum(i1, i2)
    a_lo = jnp.sum(jnp.where(eiota == e_lo, s, 0.0), axis=0, keepdims=True)
    a_hi = jnp.sum(jnp.where(eiota == e_hi, s, 0.0), axis=0, keepdims=True)
    tot = a_lo + a_hi
    liota = lax.broadcasted_iota(jnp.int32, (LANES, tm), 0)
    gates = jnp.where(liota == 0, a_lo / tot, jnp.where(liota == 1, a_hi / tot, 0.0))
    h2_ref[:, d:] = gates.T
    in_lo = jnp.bitwise_and(e_lo, EXPERTS_PER_GROUP - 1)
    in_hi = jnp.bitwise_and(e_hi, EXPERTS_PER_GROUP - 1)
    bucket = (lax.shift_right_logical(e_lo, EXPERTS_PER_GROUP.bit_length() - 1) * PAIRS_PER_GROUP
              + _pair_index(in_lo, in_hi))
    qr_ref[0:1, :] = bucket

    oh = jnp.where(lax.broadcasted_iota(jnp.int32, (base.shape[0], tm), 0) == bucket, 1.0, 0.0)
    r_i = lax.broadcasted_iota(jnp.int32, (tm, tm), 0)
    c_i = lax.broadcasted_iota(jnp.int32, (tm, tm), 1)
    upper = jnp.where(r_i <= c_i, 1.0, 0.0).astype(BF16)
    incl = _dot(oh.astype(BF16), upper)
    rank_all = base[...] + incl - oh
    qr_ref[1:2, :] = jnp.sum(oh * rank_all, axis=0, keepdims=True).astype(jnp.int32)
    new_base = base[...] + jnp.sum(oh, axis=1, keepdims=True)
    base[...] = new_base
    cnt_ref[...] = jnp.broadcast_to(new_base, cnt_ref.shape).astype(jnp.int32)


def _out_proj(l, y_conv, y_ssd, x2d, mod, norm_w, w_out, wr_hi, wr_lo, router_bias, seq):
    t, d = x2d.shape
    dc = y_conv.shape[1]
    ds = y_ssd.shape[1]
    n_exp = wr_hi.shape[0]
    tm = min(TM_PROJ, seq)
    per_b = seq // tm
    const = lambda shape: pl.BlockSpec(shape, lambda i: (0, 0))
    return pl.pallas_call(
        _outproj_kernel,
        grid=(t // tm,),
        in_specs=[
            pl.BlockSpec((tm, dc), lambda i: (i, 0)),
            pl.BlockSpec((tm, ds), lambda i: (i, 0)),
            pl.BlockSpec((tm, d), lambda i: (i, 0)),
            _mod_spec(l, 2, per_b, d), _mod_spec(l, 3, per_b, d), _mod_spec(l, 4, per_b, d),
            pl.BlockSpec((None, 1, d), lambda i: (l, 0, 0)),
            pl.BlockSpec((None, dc + ds, d), lambda i: (l, 0, 0), pipeline_mode=pl.Buffered(1)),
            const((n_exp, d)), const((n_exp, d)), const((n_exp, 1)),
        ],
        out_specs=[
            pl.BlockSpec((tm, d), lambda i: (i, 0)),
            pl.BlockSpec((tm, d + LANES), lambda i: (i, 0)),
            pl.BlockSpec((2, tm), lambda i: (0, i)),
            const((BUCKET_ROWS, LANES)),
        ],
        out_shape=[
            jax.ShapeDtypeStruct((t, d), F32),
            jax.ShapeDtypeStruct((t, d + LANES), F32),
            jax.ShapeDtypeStruct((2, t), jnp.int32),
            jax.ShapeDtypeStruct((BUCKET_ROWS, LANES), jnp.int32),
        ],
        scratch_shapes=[pltpu.VMEM((BUCKET_ROWS, 1), F32)],
        compiler_params=_cparams(("arbitrary",)),
        name="out_proj_router",
    )(y_conv, y_ssd, x2d, mod, mod, mod, norm_w, w_out, wr_hi, wr_lo,
      router_bias.reshape(n_exp, 1).astype(F32))


def _gather_rows(idx_hbm, row0, delta, src_hbm, idx_smem, isem, buf, sem):
    copies = [pltpu.make_async_copy(idx_hbm.at[row0 + i], idx_smem.at[pl.ds(i * LANES, LANES)], isem)
              for i in range(idx_smem.shape[0] // LANES)]
    for cp in copies:
        cp.start()
    for cp in copies:
        cp.wait()
    for r in range(buf.shape[0]):
        pltpu.make_async_copy(src_hbm.at[pl.ds(idx_smem[delta + r], 1)], buf.at[pl.ds(r, 1)], sem).start()


def _wait_rows(src_hbm, buf, sem):
    pltpu.make_async_copy(src_hbm.at[pl.ds(0, buf.shape[0])], buf, sem).wait()


def _pair_index(lo, hi):
    return jnp.where(lo == 0, 0, jnp.where(lo == 1, 3, 5)) + hi - lo - 1


def _pair_members(p):
    lo = jnp.where(p < 3, 0, jnp.where(p < 5, 1, 2))
    hi = jnp.where(p < 3, p + 1, jnp.where(p < 5, p - 1, 3))
    return lo, hi


WEIGHT_EVENTS = {0: (2, 3), 1: (3, 4), 3: (0, 5), 5: (1, 6)}


def _expert_kernel(bq_ref, first_ref, src_ref, nused_ref, tok_hbm, h_hbm, wg_hbm, wu_hbm, wd_hbm, o_ref,
                   idx0, idx1, xbuf0, xbuf1, sem, isem, wsem, sg, su, sd, wg_bf, wu_bf, wd_bf, *, layer):
    s = pl.program_id(0)
    b = s - 1
    n_used = nused_ref[0]
    n_exp = wg_hbm.shape[1]
    d = o_ref.shape[1]
    idx = (idx0, idx1)
    xbuf = (xbuf0, xbuf1)
    stage = ((wg_hbm, sg, wg_bf), (wu_hbm, su, wu_bf), (wd_hbm, sd, wd_bf))

    def gather(blk, p):
        src = src_ref[blk]
        _gather_rows(tok_hbm, lax.shift_right_logical(src, LANE_SHIFT), jnp.bitwise_and(src, LANES - 1),
                     h_hbm, idx[p], isem, xbuf[p], sem.at[p])

    def weights_start(e):
        for j, (w_hbm, st, _) in enumerate(stage):
            pltpu.make_async_copy(w_hbm.at[layer, e], st, wsem.at[j]).start()

    def weights_finish(slot):
        for j, (w_hbm, st, w_bf) in enumerate(stage):
            pltpu.make_async_copy(w_hbm.at[layer, 0], st, wsem.at[j]).wait()
            w_bf[slot] = st[...].astype(BF16)

    def compute(p):
        q = bq_ref[jnp.maximum(b, 0)]
        lo, hi = _pair_members(lax.rem(q, PAIRS_PER_GROUP))
        rows = xbuf[p]
        x = rows[:, 0:d].astype(BF16)
        out = None
        for slot, col in ((lo, d), (hi, d + 1)):
            gate = _dot(x, wg_bf[slot])
            up = _dot(x, wu_bf[slot])
            hb = (gate * _sigmoid(gate) * up).astype(BF16)
            part = _dot(hb, wd_bf[slot]) * rows[:, col:col + 1]
            out = part if out is None else out + part
        o_ref[...] = out

    @pl.when(s == 0)
    def _():
        gather(0, 0)
        for e in range(2):
            weights_start(e)
            weights_finish(e)
        weights_start(2)

    bq = bq_ref[jnp.maximum(b, 0)]
    group = bq // PAIRS_PER_GROUP
    pair = lax.rem(bq, PAIRS_PER_GROUP)
    is_first = (b >= 0) & (b < n_used) & (first_ref[jnp.maximum(b, 0)] == 1)
    for ev_pair, (slot, nxt) in WEIGHT_EVENTS.items():
        e_next = group * EXPERTS_PER_GROUP + nxt

        @pl.when(is_first & (pair == ev_pair) & (group * EXPERTS_PER_GROUP + nxt - 1 < n_exp))
        def _():
            weights_finish(slot)

            @pl.when(e_next < n_exp)
            def _():
                weights_start(e_next)

    for p in range(2):
        @pl.when((b >= 0) & (b < n_used) & (lax.rem(b, 2) == p))
        def _():
            _wait_rows(h_hbm, xbuf[p], sem.at[p])
            gather(jnp.minimum(s, n_used - 1), 1 - p)
            compute(p)

        @pl.when((b == n_used) & (lax.rem(b, 2) == p))
        def _():
            _wait_rows(h_hbm, xbuf[p], sem.at[p])

    @pl.when(b >= n_used)
    def _():
        o_ref[...] = jnp.zeros(o_ref.shape, o_ref.dtype)


def _experts(l, h2x, block_bucket, block_first, block_src, n_used, tok_rows, w_gate, w_up, w_down, tm):
    t, dx = h2x.shape
    nb = block_bucket.shape[0]
    _, n_exp, d, de = w_gate.shape
    assert n_exp == N_EXPERT_GROUPS * EXPERTS_PER_GROUP and dx == d + LANES
    hbm = pl.BlockSpec(memory_space=pl.ANY)
    grid_spec = pltpu.PrefetchScalarGridSpec(
        num_scalar_prefetch=4,
        grid=(nb + 1,),
        in_specs=[hbm, hbm, hbm, hbm, hbm],
        out_specs=pl.BlockSpec((tm, d), lambda s, *_: (jnp.maximum(s - 1, 0), 0)),
        scratch_shapes=[
            pltpu.SMEM((tm + LANES,), jnp.int32),
            pltpu.SMEM((tm + LANES,), jnp.int32),
            pltpu.VMEM((tm, dx), F32),
            pltpu.VMEM((tm, dx), F32),
            pltpu.SemaphoreType.DMA((2,)),
            pltpu.SemaphoreType.DMA(()),
            pltpu.SemaphoreType.DMA((3,)),
            pltpu.VMEM((d, de), F32),
            pltpu.VMEM((d, de), F32),
            pltpu.VMEM((de, d), F32),
            pltpu.VMEM((EXPERTS_PER_GROUP, d, de), BF16),
            pltpu.VMEM((EXPERTS_PER_GROUP, d, de), BF16),
            pltpu.VMEM((EXPERTS_PER_GROUP, de, d), BF16),
        ],
    )
    return pl.pallas_call(
        functools.partial(_expert_kernel, layer=l),
        grid_spec=grid_spec,
        out_shape=jax.ShapeDtypeStruct((nb * tm, d), F32),
        compiler_params=_cparams(("arbitrary",)),
        name="experts",
    )(block_bucket, block_first, block_src, n_used, tok_rows, h2x, w_gate, w_up, w_down)


def _combine_kernel(final, pos_hbm, ys_hbm, x1_ref, g2_ref, fnw_ref, o_ref,
                    idx0, idx1, ybuf0, ybuf1, sem, isem):
    s = pl.program_id(0)
    nt = pl.num_programs(0) - 1
    b = s - 1
    tm = x1_ref.shape[0]
    idx = (idx0, idx1)
    ybuf = (ybuf0, ybuf1)
    rows_per_tile = tm // LANES

    def gather(tile, p):
        _gather_rows(pos_hbm, tile * rows_per_tile, 0, ys_hbm, idx[p], isem, ybuf[p], sem.at[p])

    def finish(p):
        x2 = x1_ref[...] + g2_ref[...] * ybuf[p][...]
        if final:
            ms = jnp.mean(x2 * x2, axis=-1, keepdims=True)
            x2 = x2 * lax.rsqrt(ms + EPS) * fnw_ref[...]
        o_ref[...] = x2

    @pl.when(s == 0)
    def _():
        gather(0, 0)

    for p in range(2):
        @pl.when((b >= 0) & (s < nt) & (lax.rem(b, 2) == p))
        def _():
            _wait_rows(ys_hbm, ybuf[p], sem.at[p])
            gather(s, 1 - p)
            finish(p)

        @pl.when((s == nt) & (lax.rem(b, 2) == p))
        def _():
            _wait_rows(ys_hbm, ybuf[p], sem.at[p])
            finish(p)


def _combine(l, pos_rows, ys, x1, mod, final_norm_w, seq, tm, final):
    t, d = x1.shape
    nt = t // tm
    per_b = seq // tm
    tile = lambda s: jnp.maximum(s - 1, 0)
    return pl.pallas_call(
        functools.partial(_combine_kernel, final),
        grid=(nt + 1,),
        in_specs=[
            pl.BlockSpec(memory_space=pl.ANY),
            pl.BlockSpec(memory_space=pl.ANY),
            pl.BlockSpec((tm, d), lambda s: (tile(s), 0)),
            pl.BlockSpec((None, None, None, 1, d), lambda s: (l, tile(s) // per_b, 5, 0, 0)),
            pl.BlockSpec((1, d), lambda s: (0, 0)),
        ],
        out_specs=pl.BlockSpec((tm, d), lambda s: (tile(s), 0)),
        out_shape=jax.ShapeDtypeStruct((t, d), F32),
        scratch_shapes=[
            pltpu.SMEM((tm,), jnp.int32),
            pltpu.SMEM((tm,), jnp.int32),
            pltpu.VMEM((tm, d), F32),
            pltpu.VMEM((tm, d), F32),
            pltpu.SemaphoreType.DMA((2,)),
            pltpu.SemaphoreType.DMA(()),
        ],
        compiler_params=_cparams(("arbitrary",)),
        name="combine",
    )(pos_rows, ys, x1, mod, final_norm_w.reshape(1, d).astype(F32))


def _routing_tables(bucket, rank, counts, tm_e):
    nq = counts.shape[0]
    t = bucket.shape[0]
    nb = t // tm_e + nq
    buckets = jnp.arange(nq, dtype=jnp.int32)
    padded = jnp.maximum((counts + tm_e - 1) // tm_e, 1) * tm_e
    pends = jnp.cumsum(padded)
    pstarts = pends - padded
    starts = jnp.cumsum(counts) - counts
    pos = jnp.sum(jnp.where(bucket[None, :] == buckets[:, None], pstarts[:, None], 0), axis=0) + rank
    n_used = (pends[-1] // tm_e).astype(jnp.int32).reshape(1)
    first_slot = jnp.arange(nb, dtype=jnp.int32) * tm_e
    block_bucket = jnp.minimum(jnp.sum(pends[None, :] <= first_slot[:, None], axis=1), nq - 1).astype(jnp.int32)
    onehot = block_bucket[:, None] == buckets[None, :]
    block_first = (jnp.sum(jnp.where(onehot, pstarts[None, :], 0), axis=1) == first_slot).astype(jnp.int32)
    shift = jnp.sum(jnp.where(onehot, (pstarts - starts)[None, :], 0), axis=1)
    block_src = jnp.clip(first_slot - shift, 0, t - 1).astype(jnp.int32)
    keys = jnp.sort(bucket * t + jnp.arange(t, dtype=jnp.int32))
    tok_sorted = keys % t
    n_rows = t // LANES + tm_e // LANES + 1
    tok_rows = jnp.zeros((n_rows * LANES,), jnp.int32).at[:t].set(tok_sorted).reshape(n_rows, LANES)
    return block_bucket, block_first, block_src, n_used, tok_rows, pos.reshape(t // LANES, LANES)


def kernel(x, c, w_mod, b_mod, norm1_w, w_in, conv_dw_w, conv_dw_b, conv_ln_w, conv_ln_b,
           ssd_conv_w, ssd_conv_b, a_log, dt_bias, d_skip, ssd_norm_w, w_out, norm2_w, w_router,
           router_bias, w_gate, w_up, w_down, final_norm_w):
    bsz, seq, d = x.shape
    depth = w_mod.shape[0]
    t = bsz * seq
    d_conv = conv_dw_w.shape[2]
    d_ssd = ssd_norm_w.shape[1]
    xbc_w = ssd_conv_w.shape[2]
    heads = a_log.shape[1]
    n_main = w_in.shape[2] - heads
    tm_e = min(TM_EXPERT, seq)
    tm_c = min(TM_COMBINE, seq)
    assert seq % SSD_CHUNK == 0 and n_main % TN_PROJ == 0 and d_conv % CONV_GROUP == 0
    assert 2 * d_conv % d_ssd == 0 and (2 * d_conv + d_ssd) % xbc_w == 0
    assert tm_e % LANES == 0 and tm_c % LANES == 0 and heads <= LANES

    row3 = lambda v: v.reshape(depth, 1, -1).astype(F32)
    lane_pad = lambda v: jnp.zeros((depth, 1, LANES), F32).at[:, 0, :heads].set(v.astype(F32))
    w_in_bf = w_in.astype(BF16)
    w_dt = jnp.zeros((depth, d, LANES), BF16).at[:, :, :heads].set(w_in_bf[:, :, n_main:])
    w_out_bf = w_out.astype(BF16)
    wr_hi, wr_lo = _split2(w_router.T)
    head_of_ch = jnp.arange(d_ssd, dtype=jnp.int32) // SSD_HEAD_DIM
    expand = (jnp.arange(LANES, dtype=jnp.int32)[:, None] == head_of_ch[None, :]).astype(BF16)
    dskip_e = jnp.repeat(d_skip.astype(F32), SSD_HEAD_DIM, axis=1).reshape(depth, 1, d_ssd)

    mod = _modulation(c, w_mod, b_mod).reshape(depth, bsz, 6, 1, d)
    x2d = x.reshape(t, d)
    for l in range(depth):
        proj, dt_raw = _in_proj(l, x2d, mod, row3(norm1_w), w_in_bf, w_dt, n_main, seq)
        y_conv = _conformer_conv(l, proj, conv_dw_w, row3(conv_dw_b), row3(conv_ln_w), row3(conv_ln_b),
                                 bsz, seq)
        y_ssd = _ssd(l, proj, dt_raw, ssd_conv_w, row3(ssd_conv_b), lane_pad(a_log), lane_pad(dt_bias),
                     dskip_e, row3(ssd_norm_w), expand, bsz, seq, 2 * d_conv)
        x1, h2x, qr, cnt = _out_proj(l, y_conv, y_ssd, x2d, mod, row3(norm2_w), w_out_bf,
                                     wr_hi, wr_lo, router_bias, seq)
        block_bucket, block_first, block_src, n_used, tok_rows, pos_rows = _routing_tables(
            qr[0], qr[1], cnt[:N_BUCKETS, 0], tm_e)
        ys = _experts(l, h2x, block_bucket, block_first, block_src, n_used, tok_rows, w_gate, w_up, w_down, tm_e)
        x2d = _combine(l, pos_rows, ys, x1, mod, final_norm_w, seq, tm_c, final=(l == depth - 1))
    return x2d.reshape(bsz, seq, d)
```

```python
import functools

import jax
import jax.numpy as jnp
from jax import lax
from jax.experimental import pallas as pl
from jax.experimental.pallas import tpu as pltpu

F32 = jnp.float32
BF16 = jnp.bfloat16
EPS = 1e-6

LANES = 128
LANE_SHIFT = LANES.bit_length() - 1
SUBLANES = 8
VMEM_LIMIT_BYTES = 56 * 1024 * 1024

CONV_GROUP = 128
CONV_HALO = 32
SSD_HEAD_DIM = 64
SSD_GROUPS = 2
SSD_STATE = 128
SSD_CHUNK = 128
SSD_HALO = 8
N_EXPERT_GROUPS = 8
EXPERTS_PER_GROUP = 4
PAIRS = ((0, 1), (0, 2), (0, 3), (1, 2), (1, 3), (2, 3))
PAIRS_PER_GROUP = len(PAIRS)
N_BUCKETS = N_EXPERT_GROUPS * PAIRS_PER_GROUP
BUCKET_ROWS = 64

TM_PROJ = 512
TL_CONV = 256
CONV_ROWS = 64
TM_EXPERT = 128
TM_COMBINE = 256
TN_MOD = 1024
TN_PROJ = 512


def _cparams(sem):
    return pltpu.CompilerParams(dimension_semantics=sem, vmem_limit_bytes=VMEM_LIMIT_BYTES)


def _sigmoid(v):
    return jax.nn.sigmoid(v)


def _split2(v):
    hi = v.astype(BF16)
    lo = (v - hi.astype(F32)).astype(BF16)
    return hi, lo


def _split3(v):
    p1 = v.astype(BF16)
    r1 = v - p1.astype(F32)
    p2 = r1.astype(BF16)
    p3 = (r1 - p2.astype(F32)).astype(BF16)
    return p1, p2, p3


def _dot(a, b):
    return jnp.dot(a, b, preferred_element_type=F32)


def _dot_nt(a, b):
    return lax.dot_general(a, b, (((1,), (1,)), ((), ())), preferred_element_type=F32)


def _mod_kernel(c_ref, w_ref, b_ref, o_ref):
    c = c_ref[...]
    nb = c.shape[0]
    ca = c * _sigmoid(c)
    c_hi, c_lo = _split2(ca)
    w_hi, w_lo = _split2(w_ref[...])
    r = _dot(jnp.concatenate([c_hi, c_lo], axis=0), w_hi)
    o_ref[...] = r[:nb] + r[nb:] + _dot(c_hi, w_lo) + b_ref[...]


def _modulation(c, w_mod, b_mod):
    depth, d, n = w_mod.shape
    bsz = c.shape[0]
    tn = min(TN_MOD, n)
    return pl.pallas_call(
        _mod_kernel,
        grid=(depth, n // tn),
        in_specs=[
            pl.BlockSpec((bsz, d), lambda l, j: (0, 0)),
            pl.BlockSpec((None, d, tn), lambda l, j: (l, 0, j)),
            pl.BlockSpec((None, 1, tn), lambda l, j: (l, 0, j)),
        ],
        out_specs=pl.BlockSpec((None, bsz, tn), lambda l, j: (l, 0, j)),
        out_shape=jax.ShapeDtypeStruct((depth, bsz, n), F32),
        compiler_params=_cparams(("arbitrary", "arbitrary")),
        name="modulation",
    )(c, w_mod, b_mod.reshape(depth, 1, n))


def _mod_spec(l, j, per_b, d):
    return pl.BlockSpec((None, None, None, 1, d), lambda i: (l, i // per_b, j, 0, 0))


def _rms_modulate(x, nw, sc, sh):
    ms = jnp.mean(x * x, axis=-1, keepdims=True)
    return (x * lax.rsqrt(ms + EPS) * nw) * (1.0 + sc) + sh


def _inproj_kernel(x_ref, sh_ref, sc_ref, nw_ref, w_ref, wdt_ref, o_ref, odt_ref):
    hb = _rms_modulate(x_ref[...], nw_ref[...], sc_ref[...], sh_ref[...]).astype(BF16)
    n_main = o_ref.shape[1]
    for c0 in range(0, n_main, TN_PROJ):
        o_ref[:, c0:c0 + TN_PROJ] = _dot(hb, w_ref[:, c0:c0 + TN_PROJ]).astype(BF16)
    odt_ref[...] = _dot(hb, wdt_ref[...])


def _in_proj(l, x2d, mod, norm_w, w_in, w_dt, n_main, seq):
    t, d = x2d.shape
    tm = min(TM_PROJ, seq)
    per_b = seq // tm
    return pl.pallas_call(
        _inproj_kernel,
        grid=(t // tm,),
        in_specs=[
            pl.BlockSpec((tm, d), lambda i: (i, 0)),
            _mod_spec(l, 0, per_b, d),
            _mod_spec(l, 1, per_b, d),
            pl.BlockSpec((None, 1, d), lambda i: (l, 0, 0)),
            pl.BlockSpec((None, d, n_main), lambda i: (l, 0, 0), pipeline_mode=pl.Buffered(1)),
            pl.BlockSpec((None, d, LANES), lambda i: (l, 0, 0), pipeline_mode=pl.Buffered(1)),
        ],
        out_specs=[
            pl.BlockSpec((tm, n_main), lambda i: (i, 0)),
            pl.BlockSpec((tm, LANES), lambda i: (i, 0)),
        ],
        out_shape=[
            jax.ShapeDtypeStruct((t, n_main), BF16),
            jax.ShapeDtypeStruct((t, LANES), F32),
        ],
        compiler_params=_cparams(("arbitrary",)),
        name="in_proj",
    )(x2d, mod, mod, norm_w, w_in, w_dt)


def _inproj_combine_kernel(pos_hbm, ys_hbm, x1_ref, g2_ref, sh_ref, sc_ref, nw_ref, w_ref, wdt_ref,
                           x2_ref, o_ref, odt_ref, idx_smem, ybuf, sem, isem):
    i = pl.program_id(0)
    last = pl.num_programs(0) - 1
    rows_per_tile = ybuf.shape[0] // LANES

    def gather(tile):
        _gather_rows(pos_hbm, tile * rows_per_tile, 0, ys_hbm, idx_smem, isem, ybuf, sem)

    @pl.when(i == 0)
    def _():
        gather(0)

    _wait_rows(ys_hbm, ybuf, sem)
    x2 = x1_ref[...] + g2_ref[...] * ybuf[...]
    x2_ref[...] = x2
    gather(jnp.minimum(i + 1, last))
    hb = _rms_modulate(x2, nw_ref[...], sc_ref[...], sh_ref[...]).astype(BF16)
    n_main = o_ref.shape[1]
    for c0 in range(0, n_main, TN_PROJ):
        o_ref[:, c0:c0 + TN_PROJ] = _dot(hb, w_ref[:, c0:c0 + TN_PROJ]).astype(BF16)
    odt_ref[...] = _dot(hb, wdt_ref[...])

    @pl.when(i == last)
    def _():
        _wait_rows(ys_hbm, ybuf, sem)


def _in_proj_combine(l, pos_rows, ys, x1, mod, norm_w, w_in, w_dt, n_main, seq):
    t, d = x1.shape
    tm = min(TM_PROJ, seq)
    per_b = seq // tm
    hbm = pl.BlockSpec(memory_space=pl.ANY)
    return pl.pallas_call(
        _inproj_combine_kernel,
        grid=(t // tm,),
        in_specs=[
            hbm, hbm,
            pl.BlockSpec((tm, d), lambda i: (i, 0)),
            _mod_spec(l - 1, 5, per_b, d),
            _mod_spec(l, 0, per_b, d),
            _mod_spec(l, 1, per_b, d),
            pl.BlockSpec((None, 1, d), lambda i: (l, 0, 0)),
            pl.BlockSpec((None, d, n_main), lambda i: (l, 0, 0), pipeline_mode=pl.Buffered(1)),
            pl.BlockSpec((None, d, LANES), lambda i: (l, 0, 0), pipeline_mode=pl.Buffered(1)),
        ],
        out_specs=[
            pl.BlockSpec((tm, d), lambda i: (i, 0)),
            pl.BlockSpec((tm, n_main), lambda i: (i, 0)),
            pl.BlockSpec((tm, LANES), lambda i: (i, 0)),
        ],
        out_shape=[
            jax.ShapeDtypeStruct((t, d), F32),
            jax.ShapeDtypeStruct((t, n_main), BF16),
            jax.ShapeDtypeStruct((t, LANES), F32),
        ],
        scratch_shapes=[
            pltpu.SMEM((tm,), jnp.int32),
            pltpu.VMEM((tm, d), F32),
            pltpu.SemaphoreType.DMA(()),
            pltpu.SemaphoreType.DMA(()),
        ],
        compiler_params=_cparams(("arbitrary",)),
        name="in_proj_combine",
    )(pos_rows, ys, x1, mod, mod, mod, norm_w, w_in, w_dt)


def _conv_kernel(a_ref, g_ref, w_ref, b_ref, lnw_ref, lnb_ref, o_ref, ubuf):
    tl, ch = o_ref.shape
    taps = w_ref.shape[0]
    n = CONV_HALO + tl
    i = pl.program_id(1)

    @pl.when(i == 0)
    def _():
        ubuf[0, 0:CONV_HALO, :] = jnp.zeros((CONV_HALO, ch), F32)

    @pl.when(i > 0)
    def _():
        ubuf[0, 0:CONV_HALO, :] = ubuf[0, tl:tl + CONV_HALO, :]

    ubuf[0, CONV_HALO:n, :] = a_ref[...].astype(F32) * _sigmoid(g_ref[...].astype(F32))
    for s in range(1, SUBLANES):
        ubuf[s, 0:n - s, :] = ubuf[0, s:n, :]

    rows = min(CONV_ROWS, tl)
    for cb in range(ch // CONV_GROUP):
        cs = slice(cb * CONV_GROUP, (cb + 1) * CONV_GROUP)
        bias = b_ref[:, cs]
        lnw = lnw_ref[:, cs]
        lnb = lnb_ref[:, cs]
        for rs in range(tl // rows):
            r0 = CONV_HALO - (taps - 1) + rs * rows
            acc = jnp.zeros((rows, CONV_GROUP), F32)
            for k in range(taps):
                phase = (r0 + k) % SUBLANES
                a0 = r0 + k - phase
                acc = acc + w_ref[k:k + 1, cs] * ubuf[phase, a0:a0 + rows, cs]
            acc = acc + bias
            mu = jnp.mean(acc, axis=-1, keepdims=True)
            dev = acc - mu
            var = jnp.mean(dev * dev, axis=-1, keepdims=True)
            yn = dev * lax.rsqrt(var + EPS) * lnw + lnb
            o_ref[rs * rows:(rs + 1) * rows, cs] = (yn * _sigmoid(yn)).astype(o_ref.dtype)


def _conformer_conv(l, proj, dw_w, dw_b, ln_w, ln_b, bsz, seq):
    _, taps, ch = dw_w.shape
    tl = min(TL_CONV, seq)
    nl = seq // tl
    row = lambda b, i: b * nl + i
    vec = pl.BlockSpec((None, 1, ch), lambda b, i: (l, 0, 0))
    return pl.pallas_call(
        _conv_kernel,
        grid=(bsz, nl),
        in_specs=[
            pl.BlockSpec((tl, ch), lambda b, i: (row(b, i), 0)),
            pl.BlockSpec((tl, ch), lambda b, i: (row(b, i), 1)),
            pl.BlockSpec((None, taps, ch), lambda b, i: (l, 0, 0)),
            vec, vec, vec,
        ],
        out_specs=pl.BlockSpec((tl, ch), lambda b, i: (row(b, i), 0)),
        out_shape=jax.ShapeDtypeStruct((bsz * seq, ch), BF16),
        scratch_shapes=[pltpu.VMEM((SUBLANES, CONV_HALO + tl, ch), F32)],
        compiler_params=_cparams(("arbitrary", "arbitrary")),
        name="conformer_conv",
    )(proj, proj, dw_w, dw_b, ln_w, ln_b)


def _ssd_kernel(z_ref, xbc_ref, dt_ref, cw_ref, cb_ref, alog_ref, dtb_ref, dskip_ref, nw_ref,
                expand_ref, o_ref, xbuf, state):
    q, d_ssd = o_ref.shape
    gw = d_ssd // SSD_GROUPS
    gn = SSD_GROUPS * SSD_STATE
    taps = cw_ref.shape[0]
    c = pl.program_id(1)

    @pl.when(c == 0)
    def _():
        xbuf[0:SSD_HALO, :] = jnp.zeros((SSD_HALO, xbuf.shape[1]), F32)
        state[...] = jnp.zeros(state.shape, F32)

    @pl.when(c > 0)
    def _():
        xbuf[0:SSD_HALO, :] = xbuf[q:q + SSD_HALO, :]

    xbuf[SSD_HALO:SSD_HALO + q, :] = xbc_ref[...].astype(F32)
    acc = jnp.zeros((q, xbuf.shape[1]), F32) + cb_ref[...]
    for k in range(taps):
        r0 = SSD_HALO - (taps - 1) + k
        acc = acc + cw_ref[k:k + 1, :] * xbuf[r0:r0 + q, :]
    xc = acc * _sigmoid(acc)
    xs = xc[:, :d_ssd]
    bm = xc[:, d_ssd:d_ssd + gn]
    cm = xc[:, d_ssd + gn:]

    dt = jax.nn.softplus(dt_ref[...] + dtb_ref[...])
    a = -jnp.exp(alog_ref[...])
    da = dt * a
    rows = lax.broadcasted_iota(jnp.int32, (q, q), 0)
    cols = lax.broadcasted_iota(jnp.int32, (q, q), 1)
    tril = rows >= cols
    tri = jnp.where(tril, 1.0, 0.0).astype(BF16)
    d1, d2, d3 = _split3(da)
    cs = _dot(tri, d1) + _dot(tri, d2) + _dot(tri, d3)
    cs_t = cs.T
    cs_last = cs[q - 1:q, :]

    expand = expand_ref[...]

    def per_channel(v):
        hi, lo = _split2(v)
        return _dot(hi, expand) + _dot(lo, expand)

    dt_e = per_channel(dt)
    ecs_e = per_channel(jnp.exp(cs))
    wds_e = per_channel(dt * jnp.exp(cs_last - cs))
    xd = (xs * dt_e).astype(BF16)
    xds = (xs * wds_e).astype(BF16)
    chunk_decay = ecs_e[q - 1:q, :]

    lane = lax.broadcasted_iota(jnp.int32, (q, LANES), 1)
    heads_per_group = gw // SSD_HEAD_DIM
    y_parts = []
    for g in range(SSD_GROUPS):
        bg = bm[:, g * SSD_STATE:(g + 1) * SSD_STATE]
        cg = cm[:, g * SSD_STATE:(g + 1) * SSD_STATE].astype(BF16)
        cb_mat = _dot_nt(cg, bg.astype(BF16))
        st = state[g]
        y_off = _dot(cg, st.astype(BF16)) * ecs_e[:, g * gw:(g + 1) * gw]
        diag_parts = []
        for pr in range(heads_per_group // 2):
            lo_col = g * gw + pr * LANES
            xd_blk = xd[:, lo_col:lo_col + LANES]
            yp = jnp.zeros((q, LANES), F32)
            for half in range(2):
                h = g * heads_per_group + 2 * pr + half
                diff = cs[:, h:h + 1] - cs_t[h:h + 1, :]
                decay = jnp.exp(jnp.where(tril, diff, -jnp.inf))
                m = (cb_mat * decay).astype(BF16)
                keep = (lane >= SSD_HEAD_DIM) if half else (lane < SSD_HEAD_DIM)
                yp = yp + _dot(m, jnp.where(keep, xd_blk, jnp.zeros_like(xd_blk)))
            diag_parts.append(yp)
        y_parts.append(jnp.concatenate(diag_parts, axis=1) + y_off)
        new = _dot(bg.T.astype(BF16), xds[:, g * gw:(g + 1) * gw])
        state[g] = st * chunk_decay[:, g * gw:(g + 1) * gw] + new

    y = jnp.concatenate(y_parts, axis=1) + dskip_ref[...] * xs
    z = z_ref[...].astype(F32)
    y = y * (z * _sigmoid(z))
    outs = []
    for g in range(SSD_GROUPS):
        yg = y[:, g * gw:(g + 1) * gw]
        outs.append(yg * lax.rsqrt(jnp.mean(yg * yg, axis=-1, keepdims=True) + EPS))
    o_ref[...] = (jnp.concatenate(outs, axis=1) * nw_ref[...]).astype(o_ref.dtype)


def _ssd(l, proj, dt_raw, conv_w, conv_b, a_log, dt_bias, dskip_e, norm_w, expand, bsz, seq, z_col0):
    _, taps, xbc_w = conv_w.shape
    d_ssd = norm_w.shape[2]
    q = SSD_CHUNK
    nc = seq // q
    gw = d_ssd // SSD_GROUPS
    row = lambda b, c: b * nc + c
    vec = lambda n: pl.BlockSpec((None, 1, n), lambda b, c: (l, 0, 0))
    return pl.pallas_call(
        _ssd_kernel,
        grid=(bsz, nc),
        in_specs=[
            pl.BlockSpec((q, d_ssd), lambda b, c: (row(b, c), z_col0 // d_ssd)),
            pl.BlockSpec((q, xbc_w), lambda b, c: (row(b, c), (z_col0 + d_ssd) // xbc_w)),
            pl.BlockSpec((q, LANES), lambda b, c: (row(b, c), 0)),
            pl.BlockSpec((None, taps, xbc_w), lambda b, c: (l, 0, 0)),
            vec(xbc_w), vec(LANES), vec(LANES), vec(d_ssd), vec(d_ssd),
            pl.BlockSpec((LANES, d_ssd), lambda b, c: (0, 0)),
        ],
        out_specs=pl.BlockSpec((q, d_ssd), lambda b, c: (row(b, c), 0)),
        out_shape=jax.ShapeDtypeStruct((bsz * seq, d_ssd), BF16),
        scratch_shapes=[
            pltpu.VMEM((SSD_HALO + q, xbc_w), F32),
            pltpu.VMEM((SSD_GROUPS, SSD_STATE, gw), F32),
        ],
        compiler_params=_cparams(("arbitrary", "arbitrary")),
        name="ssd",
    )(proj, proj, dt_raw, conv_w, conv_b, a_log, dt_bias, dskip_e, norm_w, expand)


def _outproj_kernel(yc_ref, ys_ref, x_ref, g1_ref, sh2_ref, sc2_ref, nw_ref, w_ref, wr_hi_ref,
                    wr_lo_ref, rb_ref, x1_ref, h2_ref, qr_ref, cnt_ref, base):
    tm, d = x_ref.shape
    n_exp = wr_hi_ref.shape[0]
    dc = yc_ref.shape[1]
    i = pl.program_id(0)

    @pl.when(i == 0)
    def _():
        base[...] = jnp.zeros(base.shape, F32)

    mix = _dot(yc_ref[...], w_ref[0:dc, :]) + _dot(ys_ref[...], w_ref[dc:, :])
    x1 = x_ref[...] + g1_ref[...] * mix
    x1_ref[...] = x1
    h2 = _rms_modulate(x1, nw_ref[...], sc2_ref[...], sh2_ref[...])
    h2_ref[:, 0:d] = h2

    h_hi, h_lo = _split2(h2)
    wr_hi = wr_hi_ref[...]
    logits = _dot_nt(wr_hi, h_hi) + _dot_nt(wr_lo_ref[...], h_hi) + _dot_nt(wr_hi, h_lo)
    s = _sigmoid(logits)
    s_sel = s + rb_ref[...]
    eiota = lax.broadcasted_iota(jnp.int32, s.shape, 0)
    group_of = lax.shift_right_logical(eiota, EXPERTS_PER_GROUP.bit_length() - 1)
    in_group = jnp.bitwise_and(eiota, EXPERTS_PER_GROUP - 1)
    pair_max = jnp.full(s.shape, -jnp.inf, F32)
    for j in range(1, EXPERTS_PER_GROUP):
        partner = pltpu.roll(s_sel, n_exp - j, 0)
        pair_max = jnp.maximum(pair_max, jnp.where(in_group < EXPERTS_PER_GROUP - j,
                                                   s_sel + partner, -jnp.inf))
    gmax = jnp.max(pair_max, axis=0, keepdims=True)
    best = jnp.min(jnp.where(pair_max == gmax, group_of, N_EXPERT_GROUPS), axis=0, keepdims=True)
    masked = jnp.where(group_of == best, s_sel, -jnp.inf)
    m1 = jnp.max(masked, axis=0, keepdims=True)
    i1 = jnp.min(jnp.where(masked == m1, eiota, n_exp), axis=0, keepdims=True)
    masked = jnp.where(eiota == i1, -jnp.inf, masked)
    m2 = jnp.max(masked, axis=0, keepdims=True)
    i2 = jnp.min(jnp.where(masked == m2, eiota, n_exp), axis=0, keepdims=True)
    e_lo = jnp.minimum(i1, i2)
    e_hi = jnp.maximum(i1, i2)
    a_lo = jnp.sum(jnp.where(eiota == e_lo, s, 0.0), axis=0, keepdims=True)
    a_hi = jnp.sum(jnp.where(eiota == e_hi, s, 0.0), axis=0, keepdims=True)
    tot = a_lo + a_hi
    liota = lax.broadcasted_iota(jnp.int32, (LANES, tm), 0)
    gates = jnp.where(liota == 0, a_lo / tot, jnp.where(liota == 1, a_hi / tot, 0.0))
    h2_ref[:, d:] = gates.T
    in_lo = jnp.bitwise_and(e_lo, EXPERTS_PER_GROUP - 1)
    in_hi = jnp.bitwise_and(e_hi, EXPERTS_PER_GROUP - 1)
    bucket = (lax.shift_right_logical(e_lo, EXPERTS_PER_GROUP.bit_length() - 1) * PAIRS_PER_GROUP
              + _pair_index(in_lo, in_hi))
    qr_ref[0:1, :] = bucket

    oh = jnp.where(lax.broadcasted_iota(jnp.int32, (base.shape[0], tm), 0) == bucket, 1.0, 0.0)
    r_i = lax.broadcasted_iota(jnp.int32, (tm, tm), 0)
    c_i = lax.broadcasted_iota(jnp.int32, (tm, tm), 1)
    upper = jnp.where(r_i <= c_i, 1.0, 0.0).astype(BF16)
    incl = _dot(oh.astype(BF16), upper)
    rank_all = base[...] + incl - oh
    qr_ref[1:2, :] = jnp.sum(oh * rank_all, axis=0, keepdims=True).astype(jnp.int32)
    new_base = base[...] + jnp.sum(oh, axis=1, keepdims=True)
    base[...] = new_base
    cnt_ref[...] = jnp.broadcast_to(new_base, cnt_ref.shape).astype(jnp.int32)


def _out_proj(l, y_conv, y_ssd, x2d, mod, norm_w, w_out, wr_hi, wr_lo, router_bias, seq):
    t, d = x2d.shape
    dc = y_conv.shape[1]
    ds = y_ssd.shape[1]
    n_exp = wr_hi.shape[0]
    tm = min(TM_PROJ, seq)
    per_b = seq // tm
    const = lambda shape: pl.BlockSpec(shape, lambda i: (0, 0))
    return pl.pallas_call(
        _outproj_kernel,
        grid=(t // tm,),
        in_specs=[
            pl.BlockSpec((tm, dc), lambda i: (i, 0)),
            pl.BlockSpec((tm, ds), lambda i: (i, 0)),
            pl.BlockSpec((tm, d), lambda i: (i, 0)),
            _mod_spec(l, 2, per_b, d), _mod_spec(l, 3, per_b, d), _mod_spec(l, 4, per_b, d),
            pl.BlockSpec((None, 1, d), lambda i: (l, 0, 0)),
            pl.BlockSpec((None, dc + ds, d), lambda i: (l, 0, 0), pipeline_mode=pl.Buffered(1)),
            const((n_exp, d)), const((n_exp, d)), const((n_exp, 1)),
        ],
        out_specs=[
            pl.BlockSpec((tm, d), lambda i: (i, 0)),
            pl.BlockSpec((tm, d + LANES), lambda i: (i, 0)),
            pl.BlockSpec((2, tm), lambda i: (0, i)),
            const((BUCKET_ROWS, LANES)),
        ],
        out_shape=[
            jax.ShapeDtypeStruct((t, d), F32),
            jax.ShapeDtypeStruct((t, d + LANES), F32),
            jax.ShapeDtypeStruct((2, t), jnp.int32),
            jax.ShapeDtypeStruct((BUCKET_ROWS, LANES), jnp.int32),
        ],
        scratch_shapes=[pltpu.VMEM((BUCKET_ROWS, 1), F32)],
        compiler_params=_cparams(("arbitrary",)),
        name="out_proj_router",
    )(y_conv, y_ssd, x2d, mod, mod, mod, norm_w, w_out, wr_hi, wr_lo,
      router_bias.reshape(n_exp, 1).astype(F32))


def _gather_rows(idx_hbm, row0, delta, src_hbm, idx_smem, isem, buf, sem):
    copies = [pltpu.make_async_copy(idx_hbm.at[row0 + i], idx_smem.at[pl.ds(i * LANES, LANES)], isem)
              for i in range(idx_smem.shape[0] // LANES)]
    for cp in copies:
        cp.start()
    for cp in copies:
        cp.wait()
    for r in range(buf.shape[0]):
        pltpu.make_async_copy(src_hbm.at[pl.ds(idx_smem[delta + r], 1)], buf.at[pl.ds(r, 1)], sem).start()


def _wait_rows(src_hbm, buf, sem):
    pltpu.make_async_copy(src_hbm.at[pl.ds(0, buf.shape[0])], buf, sem).wait()


def _pair_index(lo, hi):
    return jnp.where(lo == 0, 0, jnp.where(lo == 1, 3, 5)) + hi - lo - 1


def _pair_members(p):
    lo = jnp.where(p < 3, 0, jnp.where(p < 5, 1, 2))
    hi = jnp.where(p < 3, p + 1, jnp.where(p < 5, p - 1, 3))
    return lo, hi


WEIGHT_EVENTS = {0: (2, 3), 1: (3, 4), 3: (0, 5), 5: (1, 6)}


def _expert_kernel(bq_ref, first_ref, src_ref, nused_ref, tok_hbm, h_hbm, wg_hbm, wu_hbm, wd_hbm, o_ref,
                   idx_smem, xbuf0, xbuf1, xbuf2, sem, isem, wsem, sg, su, sd, wg_bf, wu_bf, wd_bf, *, layer):
    s = pl.program_id(0)
    b = s - 1
    n_used = nused_ref[0]
    n_exp = wg_hbm.shape[1]
    d = o_ref.shape[1]
    xbuf = (xbuf0, xbuf1, xbuf2)
    nbuf = len(xbuf)
    stage = ((wg_hbm, sg, wg_bf), (wu_hbm, su, wu_bf), (wd_hbm, sd, wd_bf))
    b_tab = jnp.clip(b, 0, bq_ref.shape[0] - 1)
    group = bq_ref[b_tab] // PAIRS_PER_GROUP
    pair = lax.rem(bq_ref[b_tab], PAIRS_PER_GROUP)

    def gather(blk, p):
        src = src_ref[blk]
        _gather_rows(tok_hbm, lax.shift_right_logical(src, LANE_SHIFT), jnp.bitwise_and(src, LANES - 1),
                     h_hbm, idx_smem, isem, xbuf[p], sem.at[p])

    def weights_start(e):
        for j, (w_hbm, st, _) in enumerate(stage):
            pltpu.make_async_copy(w_hbm.at[layer, e], st, wsem.at[j]).start()

    def weights_finish(slot):
        for j, (w_hbm, st, w_bf) in enumerate(stage):
            pltpu.make_async_copy(w_hbm.at[layer, 0], st, wsem.at[j]).wait()
            w_bf[slot] = st[...].astype(BF16)

    def compute(p):
        lo, hi = _pair_members(pair)
        rows = xbuf[p]
        x = rows[:, 0:d].astype(BF16)
        out = None
        for slot, col in ((lo, d), (hi, d + 1)):
            gate = _dot(x, wg_bf[slot])
            up = _dot(x, wu_bf[slot])
            hb = (gate * _sigmoid(gate) * up).astype(BF16)
            part = _dot(hb, wd_bf[slot]) * rows[:, col:col + 1]
            out = part if out is None else out + part
        o_ref[...] = out

    @pl.when(s == 0)
    def _():
        gather(0, 0)
        gather(jnp.minimum(1, n_used - 1), 1)
        for e in range(2):
            weights_start(e)
            weights_finish(e)
        weights_start(2)

    is_first = (b >= 0) & (b < n_used) & (first_ref[b_tab] == 1)
    for ev_pair, (slot, nxt) in WEIGHT_EVENTS.items():
        e_next = group * EXPERTS_PER_GROUP + nxt

        @pl.when(is_first & (pair == ev_pair) & (group * EXPERTS_PER_GROUP + nxt - 1 < n_exp))
        def _():
            weights_finish(slot)

            @pl.when(e_next < n_exp)
            def _():
                weights_start(e_next)

    for p in range(nbuf):
        @pl.when((b >= 0) & (b < n_used) & (lax.rem(b, nbuf) == p))
        def _():
            _wait_rows(h_hbm, xbuf[p], sem.at[p])
            gather(jnp.minimum(s + 1, n_used - 1), (p + 2) % nbuf)
            compute(p)

        @pl.when((b >= n_used) & (b < n_used + 2) & (lax.rem(b, nbuf) == p))
        def _():
            _wait_rows(h_hbm, xbuf[p], sem.at[p])

    @pl.when(b >= n_used)
    def _():
        o_ref[...] = jnp.zeros(o_ref.shape, o_ref.dtype)


def _experts(l, h2x, block_bucket, block_first, block_src, n_used, tok_rows, w_gate, w_up, w_down, tm):
    t, dx = h2x.shape
    nb = block_bucket.shape[0]
    _, n_exp, d, de = w_gate.shape
    assert n_exp == N_EXPERT_GROUPS * EXPERTS_PER_GROUP and dx == d + LANES
    hbm = pl.BlockSpec(memory_space=pl.ANY)
    grid_spec = pltpu.PrefetchScalarGridSpec(
        num_scalar_prefetch=4,
        grid=(nb + 2,),
        in_specs=[hbm, hbm, hbm, hbm, hbm],
        out_specs=pl.BlockSpec((tm, d), lambda s, *_: (jnp.clip(s - 1, 0, nb - 1), 0)),
        scratch_shapes=[
            pltpu.SMEM((tm + LANES,), jnp.int32),
            pltpu.VMEM((tm, dx), F32),
            pltpu.VMEM((tm, dx), F32),
            pltpu.VMEM((tm, dx), F32),
            pltpu.SemaphoreType.DMA((3,)),
            pltpu.SemaphoreType.DMA(()),
            pltpu.SemaphoreType.DMA((3,)),
            pltpu.VMEM((d, de), F32),
            pltpu.VMEM((d, de), F32),
            pltpu.VMEM((de, d), F32),
            pltpu.VMEM((EXPERTS_PER_GROUP, d, de), BF16),
            pltpu.VMEM((EXPERTS_PER_GROUP, d, de), BF16),
            pltpu.VMEM((EXPERTS_PER_GROUP, de, d), BF16),
        ],
    )
    return pl.pallas_call(
        functools.partial(_expert_kernel, layer=l),
        grid_spec=grid_spec,
        out_shape=jax.ShapeDtypeStruct((nb * tm, d), F32),
        compiler_params=_cparams(("arbitrary",)),
        name="experts",
    )(block_bucket, block_first, block_src, n_used, tok_rows, h2x, w_gate, w_up, w_down)


def _combine_kernel(pos_hbm, ys_hbm, x1_ref, g2_ref, fnw_ref, o_ref,
                    idx0, idx1, ybuf0, ybuf1, sem, isem):
    s = pl.program_id(0)
    nt = pl.num_programs(0) - 1
    b = s - 1
    tm = x1_ref.shape[0]
    idx = (idx0, idx1)
    ybuf = (ybuf0, ybuf1)
    rows_per_tile = tm // LANES

    def gather(tile, p):
        _gather_rows(pos_hbm, tile * rows_per_tile, 0, ys_hbm, idx[p], isem, ybuf[p], sem.at[p])

    def finish(p):
        x2 = x1_ref[...] + g2_ref[...] * ybuf[p][...]
        ms = jnp.mean(x2 * x2, axis=-1, keepdims=True)
        o_ref[...] = x2 * lax.rsqrt(ms + EPS) * fnw_ref[...]

    @pl.when(s == 0)
    def _():
        gather(0, 0)

    for p in range(2):
        @pl.when((b >= 0) & (s < nt) & (lax.rem(b, 2) == p))
        def _():
            _wait_rows(ys_hbm, ybuf[p], sem.at[p])
            gather(s, 1 - p)
            finish(p)

        @pl.when((s == nt) & (lax.rem(b, 2) == p))
        def _():
            _wait_rows(ys_hbm, ybuf[p], sem.at[p])
            finish(p)


def _combine(l, pos_rows, ys, x1, mod, final_norm_w, seq, tm):
    t, d = x1.shape
    nt = t // tm
    per_b = seq // tm
    tile = lambda s: jnp.maximum(s - 1, 0)
    return pl.pallas_call(
        _combine_kernel,
        grid=(nt + 1,),
        in_specs=[
            pl.BlockSpec(memory_space=pl.ANY),
            pl.BlockSpec(memory_space=pl.ANY),
            pl.BlockSpec((tm, d), lambda s: (tile(s), 0)),
            pl.BlockSpec((None, None, None, 1, d), lambda s: (l, tile(s) // per_b, 5, 0, 0)),
            pl.BlockSpec((1, d), lambda s: (0, 0)),
        ],
        out_specs=pl.BlockSpec((tm, d), lambda s: (tile(s), 0)),
        out_shape=jax.ShapeDtypeStruct((t, d), F32),
        scratch_shapes=[
            pltpu.SMEM((tm,), jnp.int32),
            pltpu.SMEM((tm,), jnp.int32),
            pltpu.VMEM((tm, d), F32),
            pltpu.VMEM((tm, d), F32),
            pltpu.SemaphoreType.DMA((2,)),
            pltpu.SemaphoreType.DMA(()),
        ],
        compiler_params=_cparams(("arbitrary",)),
        name="combine",
    )(pos_rows, ys, x1, mod, final_norm_w.reshape(1, d).astype(F32))


def _routing_tables(bucket, rank, counts, tm_e):
    nq = counts.shape[0]
    t = bucket.shape[0]
    nb = t // tm_e + nq
    buckets = jnp.arange(nq, dtype=jnp.int32)
    padded = jnp.maximum((counts + tm_e - 1) // tm_e, 1) * tm_e
    pends = jnp.cumsum(padded)
    pstarts = pends - padded
    starts = jnp.cumsum(counts) - counts
    pos = jnp.sum(jnp.where(bucket[None, :] == buckets[:, None], pstarts[:, None], 0), axis=0) + rank
    n_used = (pends[-1] // tm_e).astype(jnp.int32).reshape(1)
    first_slot = jnp.arange(nb, dtype=jnp.int32) * tm_e
    block_bucket = jnp.minimum(jnp.sum(pends[None, :] <= first_slot[:, None], axis=1), nq - 1).astype(jnp.int32)
    onehot = block_bucket[:, None] == buckets[None, :]
    block_first = (jnp.sum(jnp.where(onehot, pstarts[None, :], 0), axis=1) == first_slot).astype(jnp.int32)
    shift = jnp.sum(jnp.where(onehot, (pstarts - starts)[None, :], 0), axis=1)
    block_src = jnp.clip(first_slot - shift, 0, t - 1).astype(jnp.int32)
    keys = jnp.sort(bucket * t + jnp.arange(t, dtype=jnp.int32))
    tok_sorted = keys % t
    n_rows = t // LANES + tm_e // LANES + 1
    tok_rows = jnp.zeros((n_rows * LANES,), jnp.int32).at[:t].set(tok_sorted).reshape(n_rows, LANES)
    return block_bucket, block_first, block_src, n_used, tok_rows, pos.reshape(t // LANES, LANES)


def kernel(x, c, w_mod, b_mod, norm1_w, w_in, conv_dw_w, conv_dw_b, conv_ln_w, conv_ln_b,
           ssd_conv_w, ssd_conv_b, a_log, dt_bias, d_skip, ssd_norm_w, w_out, norm2_w, w_router,
           router_bias, w_gate, w_up, w_down, final_norm_w):
    bsz, seq, d = x.shape
    depth = w_mod.shape[0]
    t = bsz * seq
    d_conv = conv_dw_w.shape[2]
    d_ssd = ssd_norm_w.shape[1]
    xbc_w = ssd_conv_w.shape[2]
    heads = a_log.shape[1]
    n_main = w_in.shape[2] - heads
    tm_e = min(TM_EXPERT, seq)
    tm_c = min(TM_COMBINE, seq)
    assert seq % SSD_CHUNK == 0 and n_main % TN_PROJ == 0 and d_conv % CONV_GROUP == 0
    assert 2 * d_conv % d_ssd == 0 and (2 * d_conv + d_ssd) % xbc_w == 0
    assert tm_e % LANES == 0 and tm_c % LANES == 0 and heads <= LANES

    row3 = lambda v: v.reshape(depth, 1, -1).astype(F32)
    lane_pad = lambda v: jnp.zeros((depth, 1, LANES), F32).at[:, 0, :heads].set(v.astype(F32))
    w_in_bf = w_in.astype(BF16)
    w_dt = jnp.zeros((depth, d, LANES), BF16).at[:, :, :heads].set(w_in_bf[:, :, n_main:])
    w_out_bf = w_out.astype(BF16)
    wr_hi, wr_lo = _split2(w_router.T)
    head_of_ch = jnp.arange(d_ssd, dtype=jnp.int32) // SSD_HEAD_DIM
    expand = (jnp.arange(LANES, dtype=jnp.int32)[:, None] == head_of_ch[None, :]).astype(BF16)
    dskip_e = jnp.repeat(d_skip.astype(F32), SSD_HEAD_DIM, axis=1).reshape(depth, 1, d_ssd)

    mod = _modulation(c, w_mod, b_mod).reshape(depth, bsz, 6, 1, d)
    x2d = x.reshape(t, d)
    for l in range(depth):
        if l == 0:
            proj, dt_raw = _in_proj(l, x2d, mod, row3(norm1_w), w_in_bf, w_dt, n_main, seq)
        else:
            x2d, proj, dt_raw = _in_proj_combine(l, pos_rows, ys, x1, mod, row3(norm1_w), w_in_bf, w_dt,
                                                 n_main, seq)
        y_conv = _conformer_conv(l, proj, conv_dw_w, row3(conv_dw_b), row3(conv_ln_w), row3(conv_ln_b),
                                 bsz, seq)
        y_ssd = _ssd(l, proj, dt_raw, ssd_conv_w, row3(ssd_conv_b), lane_pad(a_log), lane_pad(dt_bias),
                     dskip_e, row3(ssd_norm_w), expand, bsz, seq, 2 * d_conv)
        x1, h2x, qr, cnt = _out_proj(l, y_conv, y_ssd, x2d, mod, row3(norm2_w), w_out_bf,
                                     wr_hi, wr_lo, router_bias, seq)
        block_bucket, block_first, block_src, n_used, tok_rows, pos_rows = _routing_tables(
            qr[0], qr[1], cnt[:N_BUCKETS, 0], tm_e)
        ys = _experts(l, h2x, block_bucket, block_first, block_src, n_used, tok_rows, w_gate, w_up, w_down, tm_e)
    out = _combine(depth - 1, pos_rows, ys, x1, mod, final_norm_w, seq, tm_c)
    return out.reshape(bsz, seq, d)
```

```python
import functools

import jax
import jax.numpy as jnp
from jax import lax
from jax.experimental import pallas as pl
from jax.experimental.pallas import tpu as pltpu

F32 = jnp.float32
BF16 = jnp.bfloat16
EPS = 1e-6

LANES = 128
LANE_SHIFT = LANES.bit_length() - 1
SUBLANES = 8
VMEM_LIMIT_BYTES = 56 * 1024 * 1024

CONV_GROUP = 128
CONV_HALO = 32
SSD_HEAD_DIM = 64
SSD_GROUPS = 2
SSD_STATE = 128
SSD_CHUNK = 128
SSD_HALO = 8
N_EXPERT_GROUPS = 8
EXPERTS_PER_GROUP = 4
PAIRS = ((0, 1), (0, 2), (0, 3), (1, 2), (1, 3), (2, 3))
PAIRS_PER_GROUP = len(PAIRS)
N_BUCKETS = N_EXPERT_GROUPS * PAIRS_PER_GROUP
BUCKET_ROWS = 64

TM_PROJ = 512
TL_CONV = 256
CONV_ROWS = 64
TM_EXPERT = 128
TM_COMBINE = 256
TN_MOD = 1024
TN_PROJ = 512


def _cparams(sem):
    return pltpu.CompilerParams(dimension_semantics=sem, vmem_limit_bytes=VMEM_LIMIT_BYTES)


def _sigmoid(v):
    return jax.nn.sigmoid(v)


def _split2(v):
    hi = v.astype(BF16)
    lo = (v - hi.astype(F32)).astype(BF16)
    return hi, lo


def _split3(v):
    p1 = v.astype(BF16)
    r1 = v - p1.astype(F32)
    p2 = r1.astype(BF16)
    p3 = (r1 - p2.astype(F32)).astype(BF16)
    return p1, p2, p3


def _dot(a, b):
    return jnp.dot(a, b, preferred_element_type=F32)


def _dot_nt(a, b):
    return lax.dot_general(a, b, (((1,), (1,)), ((), ())), preferred_element_type=F32)


def _mod_kernel(c_ref, w_ref, b_ref, o_ref):
    c = c_ref[...]
    nb = c.shape[0]
    ca = c * _sigmoid(c)
    c_hi, c_lo = _split2(ca)
    w_hi, w_lo = _split2(w_ref[...])
    r = _dot(jnp.concatenate([c_hi, c_lo], axis=0), w_hi)
    o_ref[...] = r[:nb] + r[nb:] + _dot(c_hi, w_lo) + b_ref[...]


def _modulation(c, w_mod, b_mod):
    depth, d, n = w_mod.shape
    bsz = c.shape[0]
    tn = min(TN_MOD, n)
    return pl.pallas_call(
        _mod_kernel,
        grid=(depth, n // tn),
        in_specs=[
            pl.BlockSpec((bsz, d), lambda l, j: (0, 0)),
            pl.BlockSpec((None, d, tn), lambda l, j: (l, 0, j)),
            pl.BlockSpec((None, 1, tn), lambda l, j: (l, 0, j)),
        ],
        out_specs=pl.BlockSpec((None, bsz, tn), lambda l, j: (l, 0, j)),
        out_shape=jax.ShapeDtypeStruct((depth, bsz, n), F32),
        compiler_params=_cparams(("arbitrary", "arbitrary")),
        name="modulation",
    )(c, w_mod, b_mod.reshape(depth, 1, n))


def _mod_spec(l, j, per_b, d):
    return pl.BlockSpec((None, None, None, 1, d), lambda i: (l, i // per_b, j, 0, 0))


def _rms_modulate(x, nw, sc, sh):
    ms = jnp.mean(x * x, axis=-1, keepdims=True)
    return (x * lax.rsqrt(ms + EPS) * nw) * (1.0 + sc) + sh


def _inproj_kernel(x_ref, sh_ref, sc_ref, nw_ref, w_ref, wdt_ref, o_ref, odt_ref):
    hb = _rms_modulate(x_ref[...], nw_ref[...], sc_ref[...], sh_ref[...]).astype(BF16)
    n_main = o_ref.shape[1]
    for c0 in range(0, n_main, TN_PROJ):
        o_ref[:, c0:c0 + TN_PROJ] = _dot(hb, w_ref[:, c0:c0 + TN_PROJ]).astype(BF16)
    odt_ref[...] = _dot(hb, wdt_ref[...])


def _in_proj(l, x2d, mod, norm_w, w_in, w_dt, n_main, seq):
    t, d = x2d.shape
    tm = min(TM_PROJ, seq)
    per_b = seq // tm
    return pl.pallas_call(
        _inproj_kernel,
        grid=(t // tm,),
        in_specs=[
            pl.BlockSpec((tm, d), lambda i: (i, 0)),
            _mod_spec(l, 0, per_b, d),
            _mod_spec(l, 1, per_b, d),
            pl.BlockSpec((None, 1, d), lambda i: (l, 0, 0)),
            pl.BlockSpec((None, d, n_main), lambda i: (l, 0, 0), pipeline_mode=pl.Buffered(1)),
            pl.BlockSpec((None, d, LANES), lambda i: (l, 0, 0), pipeline_mode=pl.Buffered(1)),
        ],
        out_specs=[
            pl.BlockSpec((tm, n_main), lambda i: (i, 0)),
            pl.BlockSpec((tm, LANES), lambda i: (i, 0)),
        ],
        out_shape=[
            jax.ShapeDtypeStruct((t, n_main), BF16),
            jax.ShapeDtypeStruct((t, LANES), F32),
        ],
        compiler_params=_cparams(("arbitrary",)),
        name="in_proj",
    )(x2d, mod, mod, norm_w, w_in, w_dt)


def _inproj_combine_kernel(pos_hbm, ys_hbm, x1_ref, g2_ref, sh_ref, sc_ref, nw_ref, w_ref, wdt_ref,
                           x2_ref, o_ref, odt_ref, idx_smem, ybuf, sem, isem):
    i = pl.program_id(0)
    last = pl.num_programs(0) - 1
    rows_per_tile = ybuf.shape[0] // LANES

    def gather(tile):
        _gather_rows(pos_hbm, tile * rows_per_tile, 0, ys_hbm, idx_smem, isem, ybuf, sem)

    @pl.when(i == 0)
    def _():
        gather(0)

    _wait_rows(ys_hbm, ybuf, sem)
    x2 = x1_ref[...] + g2_ref[...] * ybuf[...]
    x2_ref[...] = x2
    gather(jnp.minimum(i + 1, last))
    hb = _rms_modulate(x2, nw_ref[...], sc_ref[...], sh_ref[...]).astype(BF16)
    n_main = o_ref.shape[1]
    for c0 in range(0, n_main, TN_PROJ):
        o_ref[:, c0:c0 + TN_PROJ] = _dot(hb, w_ref[:, c0:c0 + TN_PROJ]).astype(BF16)
    odt_ref[...] = _dot(hb, wdt_ref[...])

    @pl.when(i == last)
    def _():
        _wait_rows(ys_hbm, ybuf, sem)


def _in_proj_combine(l, pos_rows, ys, x1, mod, norm_w, w_in, w_dt, n_main, seq):
    t, d = x1.shape
    tm = min(TM_PROJ, seq)
    per_b = seq // tm
    hbm = pl.BlockSpec(memory_space=pl.ANY)
    return pl.pallas_call(
        _inproj_combine_kernel,
        grid=(t // tm,),
        in_specs=[
            hbm, hbm,
            pl.BlockSpec((tm, d), lambda i: (i, 0)),
            _mod_spec(l - 1, 5, per_b, d),
            _mod_spec(l, 0, per_b, d),
            _mod_spec(l, 1, per_b, d),
            pl.BlockSpec((None, 1, d), lambda i: (l, 0, 0)),
            pl.BlockSpec((None, d, n_main), lambda i: (l, 0, 0), pipeline_mode=pl.Buffered(1)),
            pl.BlockSpec((None, d, LANES), lambda i: (l, 0, 0), pipeline_mode=pl.Buffered(1)),
        ],
        out_specs=[
            pl.BlockSpec((tm, d), lambda i: (i, 0)),
            pl.BlockSpec((tm, n_main), lambda i: (i, 0)),
            pl.BlockSpec((tm, LANES), lambda i: (i, 0)),
        ],
        out_shape=[
            jax.ShapeDtypeStruct((t, d), F32),
            jax.ShapeDtypeStruct((t, n_main), BF16),
            jax.ShapeDtypeStruct((t, LANES), F32),
        ],
        scratch_shapes=[
            pltpu.SMEM((tm,), jnp.int32),
            pltpu.VMEM((tm, d), F32),
            pltpu.SemaphoreType.DMA(()),
            pltpu.SemaphoreType.DMA(()),
        ],
        compiler_params=_cparams(("arbitrary",)),
        name="in_proj_combine",
    )(pos_rows, ys, x1, mod, mod, mod, norm_w, w_in, w_dt)


def _conv_kernel(a_ref, g_ref, w_ref, b_ref, lnw_ref, lnb_ref, o_ref, ubuf):
    tl, ch = o_ref.shape
    taps = w_ref.shape[0]
    n = CONV_HALO + tl
    i = pl.program_id(1)

    @pl.when(i == 0)
    def _():
        ubuf[0, 0:CONV_HALO, :] = jnp.zeros((CONV_HALO, ch), F32)

    @pl.when(i > 0)
    def _():
        ubuf[0, 0:CONV_HALO, :] = ubuf[0, tl:tl + CONV_HALO, :]

    ubuf[0, CONV_HALO:n, :] = a_ref[...].astype(F32) * _sigmoid(g_ref[...].astype(F32))
    for s in range(1, SUBLANES):
        ubuf[s, 0:n - s, :] = ubuf[0, s:n, :]

    rows = min(CONV_ROWS, tl)
    for cb in range(ch // CONV_GROUP):
        cs = slice(cb * CONV_GROUP, (cb + 1) * CONV_GROUP)
        bias = b_ref[:, cs]
        lnw = lnw_ref[:, cs]
        lnb = lnb_ref[:, cs]
        for rs in range(tl // rows):
            r0 = CONV_HALO - (taps - 1) + rs * rows
            acc = jnp.zeros((rows, CONV_GROUP), F32)
            for k in range(taps):
                phase = (r0 + k) % SUBLANES
                a0 = r0 + k - phase
                acc = acc + w_ref[k:k + 1, cs] * ubuf[phase, a0:a0 + rows, cs]
            acc = acc + bias
            mu = jnp.mean(acc, axis=-1, keepdims=True)
            dev = acc - mu
            var = jnp.mean(dev * dev, axis=-1, keepdims=True)
            yn = dev * lax.rsqrt(var + EPS) * lnw + lnb
            o_ref[rs * rows:(rs + 1) * rows, cs] = (yn * _sigmoid(yn)).astype(o_ref.dtype)


def _conformer_conv(l, proj, dw_w, dw_b, ln_w, ln_b, bsz, seq):
    _, taps, ch = dw_w.shape
    tl = min(TL_CONV, seq)
    nl = seq // tl
    row = lambda b, i: b * nl + i
    vec = pl.BlockSpec((None, 1, ch), lambda b, i: (l, 0, 0))
    return pl.pallas_call(
        _conv_kernel,
        grid=(bsz, nl),
        in_specs=[
            pl.BlockSpec((tl, ch), lambda b, i: (row(b, i), 0)),
            pl.BlockSpec((tl, ch), lambda b, i: (row(b, i), 1)),
            pl.BlockSpec((None, taps, ch), lambda b, i: (l, 0, 0)),
            vec, vec, vec,
        ],
        out_specs=pl.BlockSpec((tl, ch), lambda b, i: (row(b, i), 0)),
        out_shape=jax.ShapeDtypeStruct((bsz * seq, ch), BF16),
        scratch_shapes=[pltpu.VMEM((SUBLANES, CONV_HALO + tl, ch), F32)],
        compiler_params=_cparams(("arbitrary", "arbitrary")),
        name="conformer_conv",
    )(proj, proj, dw_w, dw_b, ln_w, ln_b)


def _ssd_kernel(z_ref, xbc_ref, dt_ref, cw_ref, cb_ref, alog_ref, dtb_ref, dskip_ref, nw_ref,
                expand_ref, o_ref, xbuf, state):
    q, d_ssd = o_ref.shape
    gw = d_ssd // SSD_GROUPS
    gn = SSD_GROUPS * SSD_STATE
    taps = cw_ref.shape[0]
    c = pl.program_id(1)

    @pl.when(c == 0)
    def _():
        xbuf[0:SSD_HALO, :] = jnp.zeros((SSD_HALO, xbuf.shape[1]), F32)
        state[...] = jnp.zeros(state.shape, F32)

    @pl.when(c > 0)
    def _():
        xbuf[0:SSD_HALO, :] = xbuf[q:q + SSD_HALO, :]

    xbuf[SSD_HALO:SSD_HALO + q, :] = xbc_ref[...].astype(F32)
    acc = jnp.zeros((q, xbuf.shape[1]), F32) + cb_ref[...]
    for k in range(taps):
        r0 = SSD_HALO - (taps - 1) + k
        acc = acc + cw_ref[k:k + 1, :] * xbuf[r0:r0 + q, :]
    xc = acc * _sigmoid(acc)
    xs = xc[:, :d_ssd]
    bm = xc[:, d_ssd:d_ssd + gn]
    cm = xc[:, d_ssd + gn:]

    dt = jax.nn.softplus(dt_ref[...] + dtb_ref[...])
    a = -jnp.exp(alog_ref[...])
    da = dt * a
    rows = lax.broadcasted_iota(jnp.int32, (q, q), 0)
    cols = lax.broadcasted_iota(jnp.int32, (q, q), 1)
    tril = rows >= cols
    tri = jnp.where(tril, 1.0, 0.0).astype(BF16)
    d1, d2, d3 = _split3(da)
    cs = _dot(tri, d1) + _dot(tri, d2) + _dot(tri, d3)
    cs_t = cs.T
    cs_last = cs[q - 1:q, :]

    expand = expand_ref[...]

    def per_channel(v):
        hi, lo = _split2(v)
        return _dot(hi, expand) + _dot(lo, expand)

    dt_e = per_channel(dt)
    ecs_e = per_channel(jnp.exp(cs))
    wds_e = per_channel(dt * jnp.exp(cs_last - cs))
    xd = (xs * dt_e).astype(BF16)
    xds = (xs * wds_e).astype(BF16)
    chunk_decay = ecs_e[q - 1:q, :]

    lane = lax.broadcasted_iota(jnp.int32, (q, LANES), 1)
    heads_per_group = gw // SSD_HEAD_DIM
    y_parts = []
    for g in range(SSD_GROUPS):
        bg = bm[:, g * SSD_STATE:(g + 1) * SSD_STATE]
        cg = cm[:, g * SSD_STATE:(g + 1) * SSD_STATE].astype(BF16)
        cb_mat = _dot_nt(cg, bg.astype(BF16))
        st = state[g]
        y_off = _dot(cg, st.astype(BF16)) * ecs_e[:, g * gw:(g + 1) * gw]
        diag_parts = []
        for pr in range(heads_per_group // 2):
            lo_col = g * gw + pr * LANES
            xd_blk = xd[:, lo_col:lo_col + LANES]
            yp = jnp.zeros((q, LANES), F32)
            for half in range(2):
                h = g * heads_per_group + 2 * pr + half
                diff = cs[:, h:h + 1] - cs_t[h:h + 1, :]
                decay = jnp.exp(jnp.where(tril, diff, -jnp.inf))
                m = (cb_mat * decay).astype(BF16)
                keep = (lane >= SSD_HEAD_DIM) if half else (lane < SSD_HEAD_DIM)
                yp = yp + _dot(m, jnp.where(keep, xd_blk, jnp.zeros_like(xd_blk)))
            diag_parts.append(yp)
        y_parts.append(jnp.concatenate(diag_parts, axis=1) + y_off)
        new = _dot(bg.T.astype(BF16), xds[:, g * gw:(g + 1) * gw])
        state[g] = st * chunk_decay[:, g * gw:(g + 1) * gw] + new

    y = jnp.concatenate(y_parts, axis=1) + dskip_ref[...] * xs
    z = z_ref[...].astype(F32)
    y = y * (z * _sigmoid(z))
    outs = []
    for g in range(SSD_GROUPS):
        yg = y[:, g * gw:(g + 1) * gw]
        outs.append(yg * lax.rsqrt(jnp.mean(yg * yg, axis=-1, keepdims=True) + EPS))
    o_ref[...] = (jnp.concatenate(outs, axis=1) * nw_ref[...]).astype(o_ref.dtype)


def _ssd(l, proj, dt_raw, conv_w, conv_b, a_log, dt_bias, dskip_e, norm_w, expand, bsz, seq, z_col0):
    _, taps, xbc_w = conv_w.shape
    d_ssd = norm_w.shape[2]
    q = SSD_CHUNK
    nc = seq // q
    gw = d_ssd // SSD_GROUPS
    row = lambda b, c: b * nc + c
    vec = lambda n: pl.BlockSpec((None, 1, n), lambda b, c: (l, 0, 0))
    return pl.pallas_call(
        _ssd_kernel,
        grid=(bsz, nc),
        in_specs=[
            pl.BlockSpec((q, d_ssd), lambda b, c: (row(b, c), z_col0 // d_ssd)),
            pl.BlockSpec((q, xbc_w), lambda b, c: (row(b, c), (z_col0 + d_ssd) // xbc_w)),
            pl.BlockSpec((q, LANES), lambda b, c: (row(b, c), 0)),
            pl.BlockSpec((None, taps, xbc_w), lambda b, c: (l, 0, 0)),
            vec(xbc_w), vec(LANES), vec(LANES), vec(d_ssd), vec(d_ssd),
            pl.BlockSpec((LANES, d_ssd), lambda b, c: (0, 0)),
        ],
        out_specs=pl.BlockSpec((q, d_ssd), lambda b, c: (row(b, c), 0)),
        out_shape=jax.ShapeDtypeStruct((bsz * seq, d_ssd), BF16),
        scratch_shapes=[
            pltpu.VMEM((SSD_HALO + q, xbc_w), F32),
            pltpu.VMEM((SSD_GROUPS, SSD_STATE, gw), F32),
        ],
        compiler_params=_cparams(("arbitrary", "arbitrary")),
        name="ssd",
    )(proj, proj, dt_raw, conv_w, conv_b, a_log, dt_bias, dskip_e, norm_w, expand)


def _outproj_kernel(yc_ref, ys_ref, x_ref, g1_ref, sh2_ref, sc2_ref, nw_ref, w_ref, wr_hi_ref,
                    wr_lo_ref, rb_ref, x1_ref, h2_ref, qr_ref, cnt_ref, base):
    tm, d = x_ref.shape
    n_exp = wr_hi_ref.shape[0]
    dc = yc_ref.shape[1]
    i = pl.program_id(0)

    @pl.when(i == 0)
    def _():
        base[...] = jnp.zeros(base.shape, F32)

    mix = _dot(yc_ref[...], w_ref[0:dc, :]) + _dot(ys_ref[...], w_ref[dc:, :])
    x1 = x_ref[...] + g1_ref[...] * mix
    x1_ref[...] = x1
    h2 = _rms_modulate(x1, nw_ref[...], sc2_ref[...], sh2_ref[...])
    h2_ref[:, 0:d] = h2

    h_hi, h_lo = _split2(h2)
    wr_hi = wr_hi_ref[...]
    logits = _dot_nt(wr_hi, h_hi) + _dot_nt(wr_lo_ref[...], h_hi) + _dot_nt(wr_hi, h_lo)
    s = _sigmoid(logits)
    s_sel = s + rb_ref[...]
    eiota = lax.broadcasted_iota(jnp.int32, s.shape, 0)
    group_of = lax.shift_right_logical(eiota, EXPERTS_PER_GROUP.bit_length() - 1)
    in_group = jnp.bitwise_and(eiota, EXPERTS_PER_GROUP - 1)
    pair_max = jnp.full(s.shape, -jnp.inf, F32)
    for j in range(1, EXPERTS_PER_GROUP):
        partner = pltpu.roll(s_sel, n_exp - j, 0)
        pair_max = jnp.maximum(pair_max, jnp.where(in_group < EXPERTS_PER_GROUP - j,
                                                   s_sel + partner, -jnp.inf))
    gmax = jnp.max(pair_max, axis=0, keepdims=True)
    best = jnp.min(jnp.where(pair_max == gmax, group_of, N_EXPERT_GROUPS), axis=0, keepdims=True)
    masked = jnp.where(group_of == best, s_sel, -jnp.inf)
    m1 = jnp.max(masked, axis=0, keepdims=True)
    i1 = jnp.min(jnp.where(masked == m1, eiota, n_exp), axis=0, keepdims=True)
    masked = jnp.where(eiota == i1, -jnp.inf, masked)
    m2 = jnp.max(masked, axis=0, keepdims=True)
    i2 = jnp.min(jnp.where(masked == m2, eiota, n_exp), axis=0, keepdims=True)
    e_lo = jnp.minimum(i1, i2)
    e_hi = jnp.maximum(i1, i2)
    a_lo = jnp.sum(jnp.where(eiota == e_lo, s, 0.0), axis=0, keepdims=True)
    a_hi = jnp.sum(jnp.where(eiota == e_hi, s, 0.0), axis=0, keepdims=True)
    tot = a_lo + a_hi
    liota = lax.broadcasted_iota(jnp.int32, (LANES, tm), 0)
    gates = jnp.where(liota == 0, a_lo / tot, jnp.where(liota == 1, a_hi / tot, 0.0))
    h2_ref[:, d:] = gates.T
    in_lo = jnp.bitwise_and(e_lo, EXPERTS_PER_GROUP - 1)
    in_hi = jnp.bitwise_and(e_hi, EXPERTS_PER_GROUP - 1)
    bucket = (lax.shift_right_logical(e_lo, EXPERTS_PER_GROUP.bit_length() - 1) * PAIRS_PER_GROUP
              + _pair_index(in_lo, in_hi))
    qr_ref[0:1, :] = bucket

    oh = jnp.where(lax.broadcasted_iota(jnp.int32, (base.shape[0], tm), 0) == bucket, 1.0, 0.0)
    r_i = lax.broadcasted_iota(jnp.int32, (tm, tm), 0)
    c_i = lax.broadcasted_iota(jnp.int32, (tm, tm), 1)
    upper = jnp.where(r_i <= c_i, 1.0, 0.0).astype(BF16)
    incl = _dot(oh.astype(BF16), upper)
    rank_all = base[...] + incl - oh
    qr_ref[1:2, :] = jnp.sum(oh * rank_all, axis=0, keepdims=True).astype(jnp.int32)
    new_base = base[...] + jnp.sum(oh, axis=1, keepdims=True)
    base[...] = new_base
    cnt_ref[...] = jnp.broadcast_to(new_base, cnt_ref.shape).astype(jnp.int32)


def _out_proj(l, y_conv, y_ssd, x2d, mod, norm_w, w_out, wr_hi, wr_lo, router_bias, seq):
    t, d = x2d.shape
    dc = y_conv.shape[1]
    ds = y_ssd.shape[1]
    n_exp = wr_hi.shape[0]
    tm = min(TM_PROJ, seq)
    per_b = seq // tm
    const = lambda shape: pl.BlockSpec(shape, lambda i: (0, 0))
    return pl.pallas_call(
        _outproj_kernel,
        grid=(t // tm,),
        in_specs=[
            pl.BlockSpec((tm, dc), lambda i: (i, 0)),
            pl.BlockSpec((tm, ds), lambda i: (i, 0)),
            pl.BlockSpec((tm, d), lambda i: (i, 0)),
            _mod_spec(l, 2, per_b, d), _mod_spec(l, 3, per_b, d), _mod_spec(l, 4, per_b, d),
            pl.BlockSpec((None, 1, d), lambda i: (l, 0, 0)),
            pl.BlockSpec((None, dc + ds, d), lambda i: (l, 0, 0), pipeline_mode=pl.Buffered(1)),
            const((n_exp, d)), const((n_exp, d)), const((n_exp, 1)),
        ],
        out_specs=[
            pl.BlockSpec((tm, d), lambda i: (i, 0)),
            pl.BlockSpec((tm, d + LANES), lambda i: (i, 0)),
            pl.BlockSpec((2, tm), lambda i: (0, i)),
            const((BUCKET_ROWS, LANES)),
        ],
        out_shape=[
            jax.ShapeDtypeStruct((t, d), F32),
            jax.ShapeDtypeStruct((t, d + LANES), F32),
            jax.ShapeDtypeStruct((2, t), jnp.int32),
            jax.ShapeDtypeStruct((BUCKET_ROWS, LANES), jnp.int32),
        ],
        scratch_shapes=[pltpu.VMEM((BUCKET_ROWS, 1), F32)],
        compiler_params=_cparams(("arbitrary",)),
        name="out_proj_router",
    )(y_conv, y_ssd, x2d, mod, mod, mod, norm_w, w_out, wr_hi, wr_lo,
      router_bias.reshape(n_exp, 1).astype(F32))


def _index_copies(idx_hbm, row0, idx_smem, isem):
    return [pltpu.make_async_copy(idx_hbm.at[row0 + i], idx_smem.at[pl.ds(i * LANES, LANES)], isem)
            for i in range(idx_smem.shape[0] // LANES)]


def _start_rows(idx_smem, delta, src_hbm, buf, sem):
    for r in range(buf.shape[0]):
        pltpu.make_async_copy(src_hbm.at[pl.ds(idx_smem[delta + r], 1)], buf.at[pl.ds(r, 1)], sem).start()


def _gather_rows(idx_hbm, row0, delta, src_hbm, idx_smem, isem, buf, sem):
    copies = _index_copies(idx_hbm, row0, idx_smem, isem)
    for cp in copies:
        cp.start()
    for cp in copies:
        cp.wait()
    _start_rows(idx_smem, delta, src_hbm, buf, sem)


def _wait_rows(src_hbm, buf, sem):
    pltpu.make_async_copy(src_hbm.at[pl.ds(0, buf.shape[0])], buf, sem).wait()


def _pair_index(lo, hi):
    return jnp.where(lo == 0, 0, jnp.where(lo == 1, 3, 5)) + hi - lo - 1


def _pair_members(p):
    lo = jnp.where(p < 3, 0, jnp.where(p < 5, 1, 2))
    hi = jnp.where(p < 3, p + 1, jnp.where(p < 5, p - 1, 3))
    return lo, hi


WEIGHT_EVENTS = {0: (2, 3), 1: (3, 4), 3: (0, 5), 5: (1, 6)}


def _expert_kernel(bq_ref, first_ref, src_ref, nused_ref, tok_hbm, h_hbm, wg_hbm, wu_hbm, wd_hbm, o_ref,
                   idx0, idx1, idx2, xbuf0, xbuf1, xbuf2, sem, isem, wsem, sg, su, sd, wg_bf, wu_bf, wd_bf,
                   *, layer):
    s = pl.program_id(0)
    b = s - 1
    n_used = nused_ref[0]
    n_exp = wg_hbm.shape[1]
    d = o_ref.shape[1]
    idx = (idx0, idx1, idx2)
    xbuf = (xbuf0, xbuf1, xbuf2)
    nbuf = len(xbuf)
    stage = ((wg_hbm, sg, wg_bf), (wu_hbm, su, wu_bf), (wd_hbm, sd, wd_bf))
    b_tab = jnp.clip(b, 0, bq_ref.shape[0] - 1)
    group = bq_ref[b_tab] // PAIRS_PER_GROUP
    pair = lax.rem(bq_ref[b_tab], PAIRS_PER_GROUP)

    def index_copies(blk, p):
        src = src_ref[jnp.minimum(blk, n_used - 1)]
        return _index_copies(tok_hbm, lax.shift_right_logical(src, LANE_SHIFT), idx[p], isem)

    def start_rows(blk, p):
        src = src_ref[jnp.minimum(blk, n_used - 1)]
        _start_rows(idx[p], jnp.bitwise_and(src, LANES - 1), h_hbm, xbuf[p], sem.at[p])

    def weights_start(e):
        for j, (w_hbm, st, _) in enumerate(stage):
            pltpu.make_async_copy(w_hbm.at[layer, e], st, wsem.at[j]).start()

    def weights_finish(slot):
        for j, (w_hbm, st, w_bf) in enumerate(stage):
            pltpu.make_async_copy(w_hbm.at[layer, 0], st, wsem.at[j]).wait()
            w_bf[slot] = st[...].astype(BF16)

    def compute(p):
        lo, hi = _pair_members(pair)
        rows = xbuf[p]
        x = rows[:, 0:d].astype(BF16)
        out = None
        for slot, col in ((lo, d), (hi, d + 1)):
            gate = _dot(x, wg_bf[slot])
            up = _dot(x, wu_bf[slot])
            hb = (gate * _sigmoid(gate) * up).astype(BF16)
            part = _dot(hb, wd_bf[slot]) * rows[:, col:col + 1]
            out = part if out is None else out + part
        o_ref[...] = out

    @pl.when(s == 0)
    def _():
        for k in range(2):
            for cp in index_copies(k, k):
                cp.start()
                cp.wait()
            start_rows(k, k)
        for cp in index_copies(2, 2):
            cp.start()
        for e in range(2):
            weights_start(e)
            weights_finish(e)
        weights_start(2)

    is_first = (b >= 0) & (b < n_used) & (first_ref[b_tab] == 1)
    for ev_pair, (slot, nxt) in WEIGHT_EVENTS.items():
        e_next = group * EXPERTS_PER_GROUP + nxt

        @pl.when(is_first & (pair == ev_pair) & (group * EXPERTS_PER_GROUP + nxt - 1 < n_exp))
        def _():
            weights_finish(slot)

            @pl.when(e_next < n_exp)
            def _():
                weights_start(e_next)

    for p in range(nbuf):
        @pl.when((b >= 0) & (b < n_used) & (lax.rem(b, nbuf) == p))
        def _():
            _wait_rows(h_hbm, xbuf[p], sem.at[p])
            for cp in index_copies(s + 1, (p + 2) % nbuf):
                cp.wait()
            start_rows(s + 1, (p + 2) % nbuf)
            for cp in index_copies(s + 2, p):
                cp.start()
            compute(p)

        @pl.when((b >= n_used) & (b < n_used + 2) & (lax.rem(b, nbuf) == p))
        def _():
            _wait_rows(h_hbm, xbuf[p], sem.at[p])

            @pl.when(b == n_used)
            def _():
                for cp in index_copies(s + 1, (p + 2) % nbuf):
                    cp.wait()

    @pl.when(b >= n_used)
    def _():
        o_ref[...] = jnp.zeros(o_ref.shape, o_ref.dtype)


def _experts(l, h2x, block_bucket, block_first, block_src, n_used, tok_rows, w_gate, w_up, w_down, tm):
    t, dx = h2x.shape
    nb = block_bucket.shape[0]
    _, n_exp, d, de = w_gate.shape
    assert n_exp == N_EXPERT_GROUPS * EXPERTS_PER_GROUP and dx == d + LANES
    hbm = pl.BlockSpec(memory_space=pl.ANY)
    grid_spec = pltpu.PrefetchScalarGridSpec(
        num_scalar_prefetch=4,
        grid=(nb + 2,),
        in_specs=[hbm, hbm, hbm, hbm, hbm],
        out_specs=pl.BlockSpec((tm, d), lambda s, *_: (jnp.clip(s - 1, 0, nb - 1), 0)),
        scratch_shapes=[
            pltpu.SMEM((tm + LANES,), jnp.int32),
            pltpu.SMEM((tm + LANES,), jnp.int32),
            pltpu.SMEM((tm + LANES,), jnp.int32),
            pltpu.VMEM((tm, dx), F32),
            pltpu.VMEM((tm, dx), F32),
            pltpu.VMEM((tm, dx), F32),
            pltpu.SemaphoreType.DMA((3,)),
            pltpu.SemaphoreType.DMA(()),
            pltpu.SemaphoreType.DMA((3,)),
            pltpu.VMEM((d, de), F32),
            pltpu.VMEM((d, de), F32),
            pltpu.VMEM((de, d), F32),
            pltpu.VMEM((EXPERTS_PER_GROUP, d, de), BF16),
            pltpu.VMEM((EXPERTS_PER_GROUP, d, de), BF16),
            pltpu.VMEM((EXPERTS_PER_GROUP, de, d), BF16),
        ],
    )
    return pl.pallas_call(
        functools.partial(_expert_kernel, layer=l),
        grid_spec=grid_spec,
        out_shape=jax.ShapeDtypeStruct((nb * tm, d), F32),
        compiler_params=_cparams(("arbitrary",)),
        name="experts",
    )(block_bucket, block_first, block_src, n_used, tok_rows, h2x, w_gate, w_up, w_down)


def _combine_kernel(pos_hbm, ys_hbm, x1_ref, g2_ref, fnw_ref, o_ref,
                    idx0, idx1, ybuf0, ybuf1, sem, isem):
    s = pl.program_id(0)
    nt = pl.num_programs(0) - 1
    b = s - 1
    tm = x1_ref.shape[0]
    idx = (idx0, idx1)
    ybuf = (ybuf0, ybuf1)
    rows_per_tile = tm // LANES

    def index_copies(tile, p):
        return _index_copies(pos_hbm, jnp.minimum(tile, nt - 1) * rows_per_tile, idx[p], isem)

    def finish(p):
        x2 = x1_ref[...] + g2_ref[...] * ybuf[p][...]
        ms = jnp.mean(x2 * x2, axis=-1, keepdims=True)
        o_ref[...] = x2 * lax.rsqrt(ms + EPS) * fnw_ref[...]

    @pl.when(s == 0)
    def _():
        for cp in index_copies(0, 0):
            cp.start()
            cp.wait()
        _start_rows(idx[0], 0, ys_hbm, ybuf[0], sem.at[0])
        for cp in index_copies(1, 1):
            cp.start()

    for p in range(2):
        @pl.when((b >= 0) & (s < nt) & (lax.rem(b, 2) == p))
        def _():
            _wait_rows(ys_hbm, ybuf[p], sem.at[p])
            for cp in index_copies(s, 1 - p):
                cp.wait()
            _start_rows(idx[1 - p], 0, ys_hbm, ybuf[1 - p], sem.at[1 - p])
            for cp in index_copies(s + 1, p):
                cp.start()
            finish(p)

        @pl.when((s == nt) & (lax.rem(b, 2) == p))
        def _():
            _wait_rows(ys_hbm, ybuf[p], sem.at[p])
            for cp in index_copies(s, 1 - p):
                cp.wait()
            finish(p)


def _combine(l, pos_rows, ys, x1, mod, final_norm_w, seq, tm):
    t, d = x1.shape
    nt = t // tm
    per_b = seq // tm
    tile = lambda s: jnp.maximum(s - 1, 0)
    return pl.pallas_call(
        _combine_kernel,
        grid=(nt + 1,),
        in_specs=[
            pl.BlockSpec(memory_space=pl.ANY),
            pl.BlockSpec(memory_space=pl.ANY),
            pl.BlockSpec((tm, d), lambda s: (tile(s), 0)),
            pl.BlockSpec((None, None, None, 1, d), lambda s: (l, tile(s) // per_b, 5, 0, 0)),
            pl.BlockSpec((1, d), lambda s: (0, 0)),
        ],
        out_specs=pl.BlockSpec((tm, d), lambda s: (tile(s), 0)),
        out_shape=jax.ShapeDtypeStruct((t, d), F32),
        scratch_shapes=[
            pltpu.SMEM((tm,), jnp.int32),
            pltpu.SMEM((tm,), jnp.int32),
            pltpu.VMEM((tm, d), F32),
            pltpu.VMEM((tm, d), F32),
            pltpu.SemaphoreType.DMA((2,)),
            pltpu.SemaphoreType.DMA(()),
        ],
        compiler_params=_cparams(("arbitrary",)),
        name="combine",
    )(pos_rows, ys, x1, mod, final_norm_w.reshape(1, d).astype(F32))


def _routing_tables(bucket, rank, counts, tm_e):
    nq = counts.shape[0]
    t = bucket.shape[0]
    nb = t // tm_e + nq
    buckets = jnp.arange(nq, dtype=jnp.int32)
    padded = jnp.maximum((counts + tm_e - 1) // tm_e, 1) * tm_e
    pends = jnp.cumsum(padded)
    pstarts = pends - padded
    starts = jnp.cumsum(counts) - counts
    pos = jnp.sum(jnp.where(bucket[None, :] == buckets[:, None], pstarts[:, None], 0), axis=0) + rank
    n_used = (pends[-1] // tm_e).astype(jnp.int32).reshape(1)
    first_slot = jnp.arange(nb, dtype=jnp.int32) * tm_e
    block_bucket = jnp.minimum(jnp.sum(pends[None, :] <= first_slot[:, None], axis=1), nq - 1).astype(jnp.int32)
    onehot = block_bucket[:, None] == buckets[None, :]
    block_first = (jnp.sum(jnp.where(onehot, pstarts[None, :], 0), axis=1) == first_slot).astype(jnp.int32)
    shift = jnp.sum(jnp.where(onehot, (pstarts - starts)[None, :], 0), axis=1)
    block_src = jnp.clip(first_slot - shift, 0, t - 1).astype(jnp.int32)
    keys = jnp.sort(bucket * t + jnp.arange(t, dtype=jnp.int32))
    tok_sorted = keys % t
    n_rows = t // LANES + tm_e // LANES + 1
    tok_rows = jnp.zeros((n_rows * LANES,), jnp.int32).at[:t].set(tok_sorted).reshape(n_rows, LANES)
    return block_bucket, block_first, block_src, n_used, tok_rows, pos.reshape(t // LANES, LANES)


def kernel(x, c, w_mod, b_mod, norm1_w, w_in, conv_dw_w, conv_dw_b, conv_ln_w, conv_ln_b,
           ssd_conv_w, ssd_conv_b, a_log, dt_bias, d_skip, ssd_norm_w, w_out, norm2_w, w_router,
           router_bias, w_gate, w_up, w_down, final_norm_w):
    bsz, seq, d = x.shape
    depth = w_mod.shape[0]
    t = bsz * seq
    d_conv = conv_dw_w.shape[2]
    d_ssd = ssd_norm_w.shape[1]
    xbc_w = ssd_conv_w.shape[2]
    heads = a_log.shape[1]
    n_main = w_in.shape[2] - heads
    tm_e = min(TM_EXPERT, seq)
    tm_c = min(TM_COMBINE, seq)
    assert seq % SSD_CHUNK == 0 and n_main % TN_PROJ == 0 and d_conv % CONV_GROUP == 0
    assert 2 * d_conv % d_ssd == 0 and (2 * d_conv + d_ssd) % xbc_w == 0
    assert tm_e % LANES == 0 and tm_c % LANES == 0 and heads <= LANES

    row3 = lambda v: v.reshape(depth, 1, -1).astype(F32)
    lane_pad = lambda v: jnp.zeros((depth, 1, LANES), F32).at[:, 0, :heads].set(v.astype(F32))
    w_in_bf = w_in[:, :, :n_main].astype(BF16)
    w_dt = jnp.zeros((depth, d, LANES), BF16).at[:, :, :heads].set(w_in[:, :, n_main:].astype(BF16))
    w_out_bf = w_out.astype(BF16)
    wr_hi, wr_lo = _split2(w_router.T)
    head_of_ch = jnp.arange(d_ssd, dtype=jnp.int32) // SSD_HEAD_DIM
    expand = (jnp.arange(LANES, dtype=jnp.int32)[:, None] == head_of_ch[None, :]).astype(BF16)
    dskip_e = jnp.repeat(d_skip.astype(F32), SSD_HEAD_DIM, axis=1).reshape(depth, 1, d_ssd)

    mod = _modulation(c, w_mod, b_mod).reshape(depth, bsz, 6, 1, d)
    x2d = x.reshape(t, d)
    for l in range(depth):
        if l == 0:
            proj, dt_raw = _in_proj(l, x2d, mod, row3(norm1_w), w_in_bf, w_dt, n_main, seq)
        else:
            x2d, proj, dt_raw = _in_proj_combine(l, pos_rows, ys, x1, mod, row3(norm1_w), w_in_bf, w_dt,
                                                 n_main, seq)
        y_conv = _conformer_conv(l, proj, conv_dw_w, row3(conv_dw_b), row3(conv_ln_w), row3(conv_ln_b),
                                 bsz, seq)
        y_ssd = _ssd(l, proj, dt_raw, ssd_conv_w, row3(ssd_conv_b), lane_pad(a_log), lane_pad(dt_bias),
                     dskip_e, row3(ssd_norm_w), expand, bsz, seq, 2 * d_conv)
        x1, h2x, qr, cnt = _out_proj(l, y_conv, y_ssd, x2d, mod, row3(norm2_w), w_out_bf,
                                     wr_hi, wr_lo, router_bias, seq)
        block_bucket, block_first, block_src, n_used, tok_rows, pos_rows = _routing_tables(
            qr[0], qr[1], cnt[:N_BUCKETS, 0], tm_e)
        ys = _experts(l, h2x, block_bucket, block_first, block_src, n_used, tok_rows, w_gate, w_up, w_down, tm_e)
    out = _combine(depth - 1, pos_rows, ys, x1, mod, final_norm_w, seq, tm_c)
    return out.reshape(bsz, seq, d)
```

```python
import functools

import jax
import jax.numpy as jnp
from jax import lax
from jax.experimental import pallas as pl
from jax.experimental.pallas import tpu as pltpu

F32 = jnp.float32
BF16 = jnp.bfloat16
EPS = 1e-6

LANES = 128
LANE_SHIFT = LANES.bit_length() - 1
SUBLANES = 8
VMEM_LIMIT_BYTES = 56 * 1024 * 1024

CONV_GROUP = 128
CONV_HALO = 32
SSD_HEAD_DIM = 64
SSD_GROUPS = 2
SSD_STATE = 128
SSD_CHUNK = 128
SSD_HALO = 8
N_EXPERT_GROUPS = 8
EXPERTS_PER_GROUP = 4
PAIRS = ((0, 1), (0, 2), (0, 3), (1, 2), (1, 3), (2, 3))
PAIRS_PER_GROUP = len(PAIRS)
N_BUCKETS = N_EXPERT_GROUPS * PAIRS_PER_GROUP
BUCKET_ROWS = 64

TM_PROJ = 512
TL_CONV = 256
CONV_ROWS = 64
TM_EXPERT = 128
TM_COMBINE = 256
TN_MOD = 1024
TN_PROJ = 512


def _cparams(sem):
    return pltpu.CompilerParams(dimension_semantics=sem, vmem_limit_bytes=VMEM_LIMIT_BYTES)


def _sigmoid(v):
    return jax.nn.sigmoid(v)


def _split2(v):
    hi = v.astype(BF16)
    lo = (v - hi.astype(F32)).astype(BF16)
    return hi, lo


def _split3(v):
    p1 = v.astype(BF16)
    r1 = v - p1.astype(F32)
    p2 = r1.astype(BF16)
    p3 = (r1 - p2.astype(F32)).astype(BF16)
    return p1, p2, p3


def _dot(a, b):
    return jnp.dot(a, b, preferred_element_type=F32)


def _dot_nt(a, b):
    return lax.dot_general(a, b, (((1,), (1,)), ((), ())), preferred_element_type=F32)


def _mod_kernel(c_ref, w_ref, b_ref, o_ref):
    c = c_ref[...]
    nb = c.shape[0]
    ca = c * _sigmoid(c)
    c_hi, c_lo = _split2(ca)
    w_hi, w_lo = _split2(w_ref[...])
    r = _dot(jnp.concatenate([c_hi, c_lo], axis=0), w_hi)
    o_ref[...] = r[:nb] + r[nb:] + _dot(c_hi, w_lo) + b_ref[...]


def _modulation(c, w_mod, b_mod):
    depth, d, n = w_mod.shape
    bsz = c.shape[0]
    tn = min(TN_MOD, n)
    return pl.pallas_call(
        _mod_kernel,
        grid=(depth, n // tn),
        in_specs=[
            pl.BlockSpec((bsz, d), lambda l, j: (0, 0)),
            pl.BlockSpec((None, d, tn), lambda l, j: (l, 0, j)),
            pl.BlockSpec((None, 1, tn), lambda l, j: (l, 0, j)),
        ],
        out_specs=pl.BlockSpec((None, bsz, tn), lambda l, j: (l, 0, j)),
        out_shape=jax.ShapeDtypeStruct((depth, bsz, n), F32),
        compiler_params=_cparams(("arbitrary", "arbitrary")),
        name="modulation",
    )(c, w_mod, b_mod.reshape(depth, 1, n))


def _mod_spec(l, j, per_b, d):
    return pl.BlockSpec((None, None, None, 1, d), lambda i: (l, i // per_b, j, 0, 0))


def _rms_modulate(x, nw, sc, sh):
    ms = jnp.mean(x * x, axis=-1, keepdims=True)
    return (x * lax.rsqrt(ms + EPS) * nw) * (1.0 + sc) + sh


def _inproj_kernel(x_ref, sh_ref, sc_ref, nw_ref, w_ref, wdt_ref, o_ref, odt_ref):
    hb = _rms_modulate(x_ref[...], nw_ref[...], sc_ref[...], sh_ref[...]).astype(BF16)
    n_main = o_ref.shape[1]
    for c0 in range(0, n_main, TN_PROJ):
        o_ref[:, c0:c0 + TN_PROJ] = _dot(hb, w_ref[:, c0:c0 + TN_PROJ]).astype(BF16)
    odt_ref[...] = _dot(hb, wdt_ref[...])


def _in_proj(l, x2d, mod, norm_w, w_in, w_dt, n_main, seq):
    t, d = x2d.shape
    tm = min(TM_PROJ, seq)
    per_b = seq // tm
    return pl.pallas_call(
        _inproj_kernel,
        grid=(t // tm,),
        in_specs=[
            pl.BlockSpec((tm, d), lambda i: (i, 0)),
            _mod_spec(l, 0, per_b, d),
            _mod_spec(l, 1, per_b, d),
            pl.BlockSpec((None, 1, d), lambda i: (l, 0, 0)),
            pl.BlockSpec((None, d, n_main), lambda i: (l, 0, 0), pipeline_mode=pl.Buffered(1)),
            pl.BlockSpec((None, d, LANES), lambda i: (l, 0, 0), pipeline_mode=pl.Buffered(1)),
        ],
        out_specs=[
            pl.BlockSpec((tm, n_main), lambda i: (i, 0)),
            pl.BlockSpec((tm, LANES), lambda i: (i, 0)),
        ],
        out_shape=[
            jax.ShapeDtypeStruct((t, n_main), BF16),
            jax.ShapeDtypeStruct((t, LANES), F32),
        ],
        compiler_params=_cparams(("arbitrary",)),
        name="in_proj",
    )(x2d, mod, mod, norm_w, w_in, w_dt)


def _inproj_combine_kernel(pos_hbm, ys_hbm, x1_ref, g2_ref, sh_ref, sc_ref, nw_ref, w_ref, wdt_ref,
                           x2_ref, o_ref, odt_ref, idx_smem, ybuf, sem, isem):
    i = pl.program_id(0)
    last = pl.num_programs(0) - 1
    rows_per_tile = ybuf.shape[0] // LANES

    def gather(tile):
        _gather_rows(pos_hbm, tile * rows_per_tile, 0, ys_hbm, idx_smem, isem, ybuf, sem)

    @pl.when(i == 0)
    def _():
        gather(0)

    _wait_rows(ys_hbm, ybuf, sem)
    x2 = x1_ref[...] + g2_ref[...] * ybuf[...]
    x2_ref[...] = x2
    gather(jnp.minimum(i + 1, last))
    hb = _rms_modulate(x2, nw_ref[...], sc_ref[...], sh_ref[...]).astype(BF16)
    n_main = o_ref.shape[1]
    for c0 in range(0, n_main, TN_PROJ):
        o_ref[:, c0:c0 + TN_PROJ] = _dot(hb, w_ref[:, c0:c0 + TN_PROJ]).astype(BF16)
    odt_ref[...] = _dot(hb, wdt_ref[...])

    @pl.when(i == last)
    def _():
        _wait_rows(ys_hbm, ybuf, sem)


def _in_proj_combine(l, pos_rows, ys, x1, mod, norm_w, w_in, w_dt, n_main, seq):
    t, d = x1.shape
    tm = min(TM_PROJ, seq)
    per_b = seq // tm
    hbm = pl.BlockSpec(memory_space=pl.ANY)
    return pl.pallas_call(
        _inproj_combine_kernel,
        grid=(t // tm,),
        in_specs=[
            hbm, hbm,
            pl.BlockSpec((tm, d), lambda i: (i, 0)),
            _mod_spec(l - 1, 5, per_b, d),
            _mod_spec(l, 0, per_b, d),
            _mod_spec(l, 1, per_b, d),
            pl.BlockSpec((None, 1, d), lambda i: (l, 0, 0)),
            pl.BlockSpec((None, d, n_main), lambda i: (l, 0, 0), pipeline_mode=pl.Buffered(1)),
            pl.BlockSpec((None, d, LANES), lambda i: (l, 0, 0), pipeline_mode=pl.Buffered(1)),
        ],
        out_specs=[
            pl.BlockSpec((tm, d), lambda i: (i, 0)),
            pl.BlockSpec((tm, n_main), lambda i: (i, 0)),
            pl.BlockSpec((tm, LANES), lambda i: (i, 0)),
        ],
        out_shape=[
            jax.ShapeDtypeStruct((t, d), F32),
            jax.ShapeDtypeStruct((t, n_main), BF16),
            jax.ShapeDtypeStruct((t, LANES), F32),
        ],
        scratch_shapes=[
            pltpu.SMEM((tm,), jnp.int32),
            pltpu.VMEM((tm, d), F32),
            pltpu.SemaphoreType.DMA(()),
            pltpu.SemaphoreType.DMA(()),
        ],
        compiler_params=_cparams(("arbitrary",)),
        name="in_proj_combine",
    )(pos_rows, ys, x1, mod, mod, mod, norm_w, w_in, w_dt)


def _conv_kernel(a_ref, g_ref, w_ref, b_ref, lnw_ref, lnb_ref, o_ref, ubuf):
    tl, ch = o_ref.shape
    taps = w_ref.shape[0]
    n = CONV_HALO + tl
    i = pl.program_id(1)

    @pl.when(i == 0)
    def _():
        ubuf[0, 0:CONV_HALO, :] = jnp.zeros((CONV_HALO, ch), F32)

    @pl.when(i > 0)
    def _():
        ubuf[0, 0:CONV_HALO, :] = ubuf[0, tl:tl + CONV_HALO, :]

    ubuf[0, CONV_HALO:n, :] = a_ref[...].astype(F32) * _sigmoid(g_ref[...].astype(F32))
    for s in range(1, SUBLANES):
        ubuf[s, 0:n - s, :] = ubuf[0, s:n, :]

    rows = min(CONV_ROWS, tl)
    for cb in range(ch // CONV_GROUP):
        cs = slice(cb * CONV_GROUP, (cb + 1) * CONV_GROUP)
        bias = b_ref[:, cs]
        lnw = lnw_ref[:, cs]
        lnb = lnb_ref[:, cs]
        for rs in range(tl // rows):
            r0 = CONV_HALO - (taps - 1) + rs * rows
            acc = jnp.zeros((rows, CONV_GROUP), F32)
            for k in range(taps):
                phase = (r0 + k) % SUBLANES
                a0 = r0 + k - phase
                acc = acc + w_ref[k:k + 1, cs] * ubuf[phase, a0:a0 + rows, cs]
            acc = acc + bias
            mu = jnp.mean(acc, axis=-1, keepdims=True)
            dev = acc - mu
            var = jnp.mean(dev * dev, axis=-1, keepdims=True)
            yn = dev * lax.rsqrt(var + EPS) * lnw + lnb
            o_ref[rs * rows:(rs + 1) * rows, cs] = (yn * _sigmoid(yn)).astype(o_ref.dtype)


def _conformer_conv(l, proj, dw_w, dw_b, ln_w, ln_b, bsz, seq):
    _, taps, ch = dw_w.shape
    tl = min(TL_CONV, seq)
    nl = seq // tl
    row = lambda b, i: b * nl + i
    vec = pl.BlockSpec((None, 1, ch), lambda b, i: (l, 0, 0))
    return pl.pallas_call(
        _conv_kernel,
        grid=(bsz, nl),
        in_specs=[
            pl.BlockSpec((tl, ch), lambda b, i: (row(b, i), 0)),
            pl.BlockSpec((tl, ch), lambda b, i: (row(b, i), 1)),
            pl.BlockSpec((None, taps, ch), lambda b, i: (l, 0, 0)),
            vec, vec, vec,
        ],
        out_specs=pl.BlockSpec((tl, ch), lambda b, i: (row(b, i), 0)),
        out_shape=jax.ShapeDtypeStruct((bsz * seq, ch), BF16),
        scratch_shapes=[pltpu.VMEM((SUBLANES, CONV_HALO + tl, ch), F32)],
        compiler_params=_cparams(("arbitrary", "arbitrary")),
        name="conformer_conv",
    )(proj, proj, dw_w, dw_b, ln_w, ln_b)


def _ssd_kernel(z_ref, xbc_ref, dt_ref, cw_ref, cb_ref, alog_ref, dtb_ref, dskip_ref, nw_ref,
                expand_ref, o_ref, xbuf, state):
    q, d_ssd = o_ref.shape
    gw = d_ssd // SSD_GROUPS
    gn = SSD_GROUPS * SSD_STATE
    taps = cw_ref.shape[0]
    c = pl.program_id(1)

    @pl.when(c == 0)
    def _():
        xbuf[0:SSD_HALO, :] = jnp.zeros((SSD_HALO, xbuf.shape[1]), F32)
        state[...] = jnp.zeros(state.shape, F32)

    @pl.when(c > 0)
    def _():
        xbuf[0:SSD_HALO, :] = xbuf[q:q + SSD_HALO, :]

    xbuf[SSD_HALO:SSD_HALO + q, :] = xbc_ref[...].astype(F32)
    acc = jnp.zeros((q, xbuf.shape[1]), F32) + cb_ref[...]
    for k in range(taps):
        r0 = SSD_HALO - (taps - 1) + k
        acc = acc + cw_ref[k:k + 1, :] * xbuf[r0:r0 + q, :]
    xc = acc * _sigmoid(acc)
    xs = xc[:, :d_ssd]
    bm = xc[:, d_ssd:d_ssd + gn]
    cm = xc[:, d_ssd + gn:]

    dt = jax.nn.softplus(dt_ref[...] + dtb_ref[...])
    a = -jnp.exp(alog_ref[...])
    da = dt * a
    rows = lax.broadcasted_iota(jnp.int32, (q, q), 0)
    cols = lax.broadcasted_iota(jnp.int32, (q, q), 1)
    tril = rows >= cols
    tri = jnp.where(tril, 1.0, 0.0).astype(BF16)
    d1, d2, d3 = _split3(da)
    cs = _dot(tri, d1) + _dot(tri, d2) + _dot(tri, d3)
    cs_t = cs.T
    cs_last = cs[q - 1:q, :]

    expand = expand_ref[...]

    def per_channel(v):
        hi, lo = _split2(v)
        return _dot(hi, expand) + _dot(lo, expand)

    dt_e = per_channel(dt)
    ecs_e = per_channel(jnp.exp(cs))
    wds_e = per_channel(dt * jnp.exp(cs_last - cs))
    xd = (xs * dt_e).astype(BF16)
    xds = (xs * wds_e).astype(BF16)
    chunk_decay = ecs_e[q - 1:q, :]

    lane = lax.broadcasted_iota(jnp.int32, (q, LANES), 1)
    heads_per_group = gw // SSD_HEAD_DIM
    y_parts = []
    for g in range(SSD_GROUPS):
        bg = bm[:, g * SSD_STATE:(g + 1) * SSD_STATE]
        cg = cm[:, g * SSD_STATE:(g + 1) * SSD_STATE].astype(BF16)
        cb_mat = _dot_nt(cg, bg.astype(BF16))
        st = state[g]
        y_off = _dot(cg, st.astype(BF16)) * ecs_e[:, g * gw:(g + 1) * gw]
        diag_parts = []
        for pr in range(heads_per_group // 2):
            lo_col = g * gw + pr * LANES
            xd_blk = xd[:, lo_col:lo_col + LANES]
            yp = jnp.zeros((q, LANES), F32)
            for half in range(2):
                h = g * heads_per_group + 2 * pr + half
                diff = cs[:, h:h + 1] - cs_t[h:h + 1, :]
                decay = jnp.exp(jnp.where(tril, diff, -jnp.inf))
                m = (cb_mat * decay).astype(BF16)
                keep = (lane >= SSD_HEAD_DIM) if half else (lane < SSD_HEAD_DIM)
                yp = yp + _dot(m, jnp.where(keep, xd_blk, jnp.zeros_like(xd_blk)))
            diag_parts.append(yp)
        y_parts.append(jnp.concatenate(diag_parts, axis=1) + y_off)
        new = _dot(bg.T.astype(BF16), xds[:, g * gw:(g + 1) * gw])
        state[g] = st * chunk_decay[:, g * gw:(g + 1) * gw] + new

    y = jnp.concatenate(y_parts, axis=1) + dskip_ref[...] * xs
    z = z_ref[...].astype(F32)
    y = y * (z * _sigmoid(z))
    outs = []
    for g in range(SSD_GROUPS):
        yg = y[:, g * gw:(g + 1) * gw]
        outs.append(yg * lax.rsqrt(jnp.mean(yg * yg, axis=-1, keepdims=True) + EPS))
    o_ref[...] = (jnp.concatenate(outs, axis=1) * nw_ref[...]).astype(o_ref.dtype)


def _ssd(l, proj, dt_raw, conv_w, conv_b, a_log, dt_bias, dskip_e, norm_w, expand, bsz, seq, z_col0):
    _, taps, xbc_w = conv_w.shape
    d_ssd = norm_w.shape[2]
    q = SSD_CHUNK
    nc = seq // q
    gw = d_ssd // SSD_GROUPS
    row = lambda b, c: b * nc + c
    vec = lambda n: pl.BlockSpec((None, 1, n), lambda b, c: (l, 0, 0))
    return pl.pallas_call(
        _ssd_kernel,
        grid=(bsz, nc),
        in_specs=[
            pl.BlockSpec((q, d_ssd), lambda b, c: (row(b, c), z_col0 // d_ssd)),
            pl.BlockSpec((q, xbc_w), lambda b, c: (row(b, c), (z_col0 + d_ssd) // xbc_w)),
            pl.BlockSpec((q, LANES), lambda b, c: (row(b, c), 0)),
            pl.BlockSpec((None, taps, xbc_w), lambda b, c: (l, 0, 0)),
            vec(xbc_w), vec(LANES), vec(LANES), vec(d_ssd), vec(d_ssd),
            pl.BlockSpec((LANES, d_ssd), lambda b, c: (0, 0)),
        ],
        out_specs=pl.BlockSpec((q, d_ssd), lambda b, c: (row(b, c), 0)),
        out_shape=jax.ShapeDtypeStruct((bsz * seq, d_ssd), BF16),
        scratch_shapes=[
            pltpu.VMEM((SSD_HALO + q, xbc_w), F32),
            pltpu.VMEM((SSD_GROUPS, SSD_STATE, gw), F32),
        ],
        compiler_params=_cparams(("arbitrary", "arbitrary")),
        name="ssd",
    )(proj, proj, dt_raw, conv_w, conv_b, a_log, dt_bias, dskip_e, norm_w, expand)


def _outproj_kernel(yc_ref, ys_ref, x_ref, g1_ref, sh2_ref, sc2_ref, nw_ref, w_ref, wr_hi_ref,
                    wr_lo_ref, rb_ref, x1_ref, h2_ref, qr_ref, cnt_ref, base):
    tm, d = x_ref.shape
    n_exp = wr_hi_ref.shape[0]
    dc = yc_ref.shape[1]
    i = pl.program_id(0)

    @pl.when(i == 0)
    def _():
        base[...] = jnp.zeros(base.shape, F32)

    mix = _dot(yc_ref[...], w_ref[0:dc, :]) + _dot(ys_ref[...], w_ref[dc:, :])
    x1 = x_ref[...] + g1_ref[...] * mix
    x1_ref[...] = x1
    h2 = _rms_modulate(x1, nw_ref[...], sc2_ref[...], sh2_ref[...])
    span = d // LANES + 1
    for cc in range(span - 1):
        h2_ref[pl.ds(cc, tm, stride=span), :] = h2[:, cc * LANES:(cc + 1) * LANES]

    h_hi, h_lo = _split2(h2)
    wr_hi = wr_hi_ref[...]
    logits = _dot_nt(wr_hi, h_hi) + _dot_nt(wr_lo_ref[...], h_hi) + _dot_nt(wr_hi, h_lo)
    s = _sigmoid(logits)
    s_sel = s + rb_ref[...]
    eiota = lax.broadcasted_iota(jnp.int32, s.shape, 0)
    group_of = lax.shift_right_logical(eiota, EXPERTS_PER_GROUP.bit_length() - 1)
    in_group = jnp.bitwise_and(eiota, EXPERTS_PER_GROUP - 1)
    pair_max = jnp.full(s.shape, -jnp.inf, F32)
    for j in range(1, EXPERTS_PER_GROUP):
        partner = pltpu.roll(s_sel, n_exp - j, 0)
        pair_max = jnp.maximum(pair_max, jnp.where(in_group < EXPERTS_PER_GROUP - j,
                                                   s_sel + partner, -jnp.inf))
    gmax = jnp.max(pair_max, axis=0, keepdims=True)
    best = jnp.min(jnp.where(pair_max == gmax, group_of, N_EXPERT_GROUPS), axis=0, keepdims=True)
    masked = jnp.where(group_of == best, s_sel, -jnp.inf)
    m1 = jnp.max(masked, axis=0, keepdims=True)
    i1 = jnp.min(jnp.where(masked == m1, eiota, n_exp), axis=0, keepdims=True)
    masked = jnp.where(eiota == i1, -jnp.inf, masked)
    m2 = jnp.max(masked, axis=0, keepdims=True)
    i2 = jnp.min(jnp.where(masked == m2, eiota, n_exp), axis=0, keepdims=True)
    e_lo = jnp.minimum(i1, i2)
    e_hi = jnp.maximum(i1, i2)
    a_lo = jnp.sum(jnp.where(eiota == e_lo, s, 0.0), axis=0, keepdims=True)
    a_hi = jnp.sum(jnp.where(eiota == e_hi, s, 0.0), axis=0, keepdims=True)
    tot = a_lo + a_hi
    liota = lax.broadcasted_iota(jnp.int32, (LANES, tm), 0)
    gates = jnp.where(liota == 0, a_lo / tot, jnp.where(liota == 1, a_hi / tot, 0.0))
    h2_ref[pl.ds(span - 1, tm, stride=span), :] = gates.T
    in_lo = jnp.bitwise_and(e_lo, EXPERTS_PER_GROUP - 1)
    in_hi = jnp.bitwise_and(e_hi, EXPERTS_PER_GROUP - 1)
    bucket = (lax.shift_right_logical(e_lo, EXPERTS_PER_GROUP.bit_length() - 1) * PAIRS_PER_GROUP
              + _pair_index(in_lo, in_hi))
    qr_ref[0:1, :] = bucket

    oh = jnp.where(lax.broadcasted_iota(jnp.int32, (base.shape[0], tm), 0) == bucket, 1.0, 0.0)
    r_i = lax.broadcasted_iota(jnp.int32, (tm, tm), 0)
    c_i = lax.broadcasted_iota(jnp.int32, (tm, tm), 1)
    upper = jnp.where(r_i <= c_i, 1.0, 0.0).astype(BF16)
    incl = _dot(oh.astype(BF16), upper)
    rank_all = base[...] + incl - oh
    qr_ref[1:2, :] = jnp.sum(oh * rank_all, axis=0, keepdims=True).astype(jnp.int32)
    new_base = base[...] + jnp.sum(oh, axis=1, keepdims=True)
    base[...] = new_base
    cnt_ref[...] = jnp.broadcast_to(new_base, cnt_ref.shape).astype(jnp.int32)


def _out_proj(l, y_conv, y_ssd, x2d, mod, norm_w, w_out, wr_hi, wr_lo, router_bias, seq):
    t, d = x2d.shape
    dc = y_conv.shape[1]
    ds = y_ssd.shape[1]
    n_exp = wr_hi.shape[0]
    tm = min(TM_PROJ, seq)
    per_b = seq // tm
    const = lambda shape: pl.BlockSpec(shape, lambda i: (0, 0))
    return pl.pallas_call(
        _outproj_kernel,
        grid=(t // tm,),
        in_specs=[
            pl.BlockSpec((tm, dc), lambda i: (i, 0)),
            pl.BlockSpec((tm, ds), lambda i: (i, 0)),
            pl.BlockSpec((tm, d), lambda i: (i, 0)),
            _mod_spec(l, 2, per_b, d), _mod_spec(l, 3, per_b, d), _mod_spec(l, 4, per_b, d),
            pl.BlockSpec((None, 1, d), lambda i: (l, 0, 0)),
            pl.BlockSpec((None, dc + ds, d), lambda i: (l, 0, 0), pipeline_mode=pl.Buffered(1)),
            const((n_exp, d)), const((n_exp, d)), const((n_exp, 1)),
        ],
        out_specs=[
            pl.BlockSpec((tm, d), lambda i: (i, 0)),
            pl.BlockSpec((tm * (d // LANES + 1), LANES), lambda i: (i, 0)),
            pl.BlockSpec((2, tm), lambda i: (0, i)),
            const((BUCKET_ROWS, LANES)),
        ],
        out_shape=[
            jax.ShapeDtypeStruct((t, d), F32),
            jax.ShapeDtypeStruct((t * (d // LANES + 1), LANES), F32),
            jax.ShapeDtypeStruct((2, t), jnp.int32),
            jax.ShapeDtypeStruct((BUCKET_ROWS, LANES), jnp.int32),
        ],
        scratch_shapes=[pltpu.VMEM((BUCKET_ROWS, 1), F32)],
        compiler_params=_cparams(("arbitrary",)),
        name="out_proj_router",
    )(y_conv, y_ssd, x2d, mod, mod, mod, norm_w, w_out, wr_hi, wr_lo,
      router_bias.reshape(n_exp, 1).astype(F32))


def _index_copies(idx_hbm, row0, idx_smem, isem):
    return [pltpu.make_async_copy(idx_hbm.at[row0 + i], idx_smem.at[pl.ds(i * LANES, LANES)], isem)
            for i in range(idx_smem.shape[0] // LANES)]


def _start_rows(idx_smem, delta, src_hbm, buf, sem, span=1):
    for r in range(buf.shape[0] // span):
        pltpu.make_async_copy(src_hbm.at[pl.ds(idx_smem[delta + r] * span, span)],
                              buf.at[pl.ds(r * span, span)], sem).start()


def _gather_rows(idx_hbm, row0, delta, src_hbm, idx_smem, isem, buf, sem):
    copies = _index_copies(idx_hbm, row0, idx_smem, isem)
    for cp in copies:
        cp.start()
    for cp in copies:
        cp.wait()
    _start_rows(idx_smem, delta, src_hbm, buf, sem)


def _wait_rows(src_hbm, buf, sem):
    pltpu.make_async_copy(src_hbm.at[pl.ds(0, buf.shape[0])], buf, sem).wait()


def _pair_index(lo, hi):
    return jnp.where(lo == 0, 0, jnp.where(lo == 1, 3, 5)) + hi - lo - 1


def _pair_members(p):
    lo = jnp.where(p < 3, 0, jnp.where(p < 5, 1, 2))
    hi = jnp.where(p < 3, p + 1, jnp.where(p < 5, p - 1, 3))
    return lo, hi


WEIGHT_EVENTS = {0: (2, 3), 1: (3, 4), 3: (0, 5), 5: (1, 6)}


def _expert_kernel(bq_ref, first_ref, src_ref, nused_ref, tok_hbm, h_hbm, wg_hbm, wu_hbm, wd_hbm, o_ref,
                   idx0, idx1, idx2, xbuf0, xbuf1, xbuf2, sem, isem, wsem, sg, su, sd, wg_bf, wu_bf, wd_bf,
                   *, layer):
    s = pl.program_id(0)
    b = s - 1
    n_used = nused_ref[0]
    n_exp = wg_hbm.shape[1]
    tm, d = o_ref.shape
    span = d // LANES + 1
    idx = (idx0, idx1, idx2)
    xbuf = (xbuf0, xbuf1, xbuf2)
    nbuf = len(xbuf)
    stage = ((wg_hbm, sg, wg_bf), (wu_hbm, su, wu_bf), (wd_hbm, sd, wd_bf))
    b_tab = jnp.clip(b, 0, bq_ref.shape[0] - 1)
    group = bq_ref[b_tab] // PAIRS_PER_GROUP
    pair = lax.rem(bq_ref[b_tab], PAIRS_PER_GROUP)

    def index_copies(blk, p):
        src = src_ref[jnp.minimum(blk, n_used - 1)]
        return _index_copies(tok_hbm, lax.shift_right_logical(src, LANE_SHIFT), idx[p], isem)

    def start_rows(blk, p):
        src = src_ref[jnp.minimum(blk, n_used - 1)]
        _start_rows(idx[p], jnp.bitwise_and(src, LANES - 1), h_hbm, xbuf[p], sem.at[p], span)

    def weights_start(e):
        for j, (w_hbm, st, _) in enumerate(stage):
            pltpu.make_async_copy(w_hbm.at[layer, e], st, wsem.at[j]).start()

    def weights_finish(slot):
        for j, (w_hbm, st, w_bf) in enumerate(stage):
            pltpu.make_async_copy(w_hbm.at[layer, 0], st, wsem.at[j]).wait()
            w_bf[slot] = st[...].astype(BF16)

    def compute(p):
        lo, hi = _pair_members(pair)
        chunks = [xbuf[p][pl.ds(cc, tm, stride=span), :] for cc in range(span)]
        x = jnp.concatenate(chunks[:-1], axis=1).astype(BF16)
        out = None
        for slot, col in ((lo, 0), (hi, 1)):
            gate = _dot(x, wg_bf[slot])
            up = _dot(x, wu_bf[slot])
            hb = (gate * _sigmoid(gate) * up).astype(BF16)
            part = _dot(hb, wd_bf[slot]) * chunks[-1][:, col:col + 1]
            out = part if out is None else out + part
        o_ref[...] = out

    @pl.when(s == 0)
    def _():
        for k in range(2):
            for cp in index_copies(k, k):
                cp.start()
                cp.wait()
            start_rows(k, k)
        for cp in index_copies(2, 2):
            cp.start()
        for e in range(2):
            weights_start(e)
            weights_finish(e)
        weights_start(2)

    is_first = (b >= 0) & (b < n_used) & (first_ref[b_tab] == 1)
    for ev_pair, (slot, nxt) in WEIGHT_EVENTS.items():
        e_next = group * EXPERTS_PER_GROUP + nxt

        @pl.when(is_first & (pair == ev_pair) & (group * EXPERTS_PER_GROUP + nxt - 1 < n_exp))
        def _():
            weights_finish(slot)

            @pl.when(e_next < n_exp)
            def _():
                weights_start(e_next)

    for p in range(nbuf):
        @pl.when((b >= 0) & (b < n_used) & (lax.rem(b, nbuf) == p))
        def _():
            for cp in index_copies(s + 1, (p + 2) % nbuf):
                cp.wait()
            start_rows(s + 1, (p + 2) % nbuf)
            for cp in index_copies(s + 2, p):
                cp.start()

        @pl.when((b >= 0) & (b < n_used) & (lax.rem(b, nbuf) == p))
        def _():
            _wait_rows(h_hbm, xbuf[p], sem.at[p])
            compute(p)

        @pl.when((b >= n_used) & (b < n_used + 2) & (lax.rem(b, nbuf) == p))
        def _():
            _wait_rows(h_hbm, xbuf[p], sem.at[p])

            @pl.when(b == n_used)
            def _():
                for cp in index_copies(s + 1, (p + 2) % nbuf):
                    cp.wait()

    @pl.when(b >= n_used)
    def _():
        o_ref[...] = jnp.zeros(o_ref.shape, o_ref.dtype)


def _experts(l, h2x, block_bucket, block_first, block_src, n_used, tok_rows, w_gate, w_up, w_down, tm):
    nb = block_bucket.shape[0]
    _, n_exp, d, de = w_gate.shape
    span = d // LANES + 1
    assert n_exp == N_EXPERT_GROUPS * EXPERTS_PER_GROUP and h2x.shape[1] == LANES and h2x.shape[0] % span == 0
    hbm = pl.BlockSpec(memory_space=pl.ANY)
    grid_spec = pltpu.PrefetchScalarGridSpec(
        num_scalar_prefetch=4,
        grid=(nb + 2,),
        in_specs=[hbm, hbm, hbm, hbm, hbm],
        out_specs=pl.BlockSpec((tm, d), lambda s, *_: (jnp.clip(s - 1, 0, nb - 1), 0)),
        scratch_shapes=[
            pltpu.SMEM((tm + LANES,), jnp.int32),
            pltpu.SMEM((tm + LANES,), jnp.int32),
            pltpu.SMEM((tm + LANES,), jnp.int32),
            pltpu.VMEM((tm * span, LANES), F32),
            pltpu.VMEM((tm * span, LANES), F32),
            pltpu.VMEM((tm * span, LANES), F32),
            pltpu.SemaphoreType.DMA((3,)),
            pltpu.SemaphoreType.DMA(()),
            pltpu.SemaphoreType.DMA((3,)),
            pltpu.VMEM((d, de), F32),
            pltpu.VMEM((d, de), F32),
            pltpu.VMEM((de, d), F32),
            pltpu.VMEM((EXPERTS_PER_GROUP, d, de), BF16),
            pltpu.VMEM((EXPERTS_PER_GROUP, d, de), BF16),
            pltpu.VMEM((EXPERTS_PER_GROUP, de, d), BF16),
        ],
    )
    return pl.pallas_call(
        functools.partial(_expert_kernel, layer=l),
        grid_spec=grid_spec,
        out_shape=jax.ShapeDtypeStruct((nb * tm, d), F32),
        compiler_params=_cparams(("arbitrary",)),
        name="experts",
    )(block_bucket, block_first, block_src, n_used, tok_rows, h2x, w_gate, w_up, w_down)


def _combine_kernel(pos_hbm, ys_hbm, x1_ref, g2_ref, fnw_ref, o_ref,
                    idx0, idx1, ybuf0, ybuf1, sem, isem):
    s = pl.program_id(0)
    nt = pl.num_programs(0) - 1
    b = s - 1
    tm = x1_ref.shape[0]
    idx = (idx0, idx1)
    ybuf = (ybuf0, ybuf1)
    rows_per_tile = tm // LANES

    def index_copies(tile, p):
        return _index_copies(pos_hbm, jnp.minimum(tile, nt - 1) * rows_per_tile, idx[p], isem)

    def finish(p):
        x2 = x1_ref[...] + g2_ref[...] * ybuf[p][...]
        ms = jnp.mean(x2 * x2, axis=-1, keepdims=True)
        o_ref[...] = x2 * lax.rsqrt(ms + EPS) * fnw_ref[...]

    @pl.when(s == 0)
    def _():
        for cp in index_copies(0, 0):
            cp.start()
            cp.wait()
        _start_rows(idx[0], 0, ys_hbm, ybuf[0], sem.at[0])
        for cp in index_copies(1, 1):
            cp.start()

    for p in range(2):
        @pl.when((b >= 0) & (s < nt) & (lax.rem(b, 2) == p))
        def _():
            _wait_rows(ys_hbm, ybuf[p], sem.at[p])
            for cp in index_copies(s, 1 - p):
                cp.wait()
            _start_rows(idx[1 - p], 0, ys_hbm, ybuf[1 - p], sem.at[1 - p])
            for cp in index_copies(s + 1, p):
                cp.start()
            finish(p)

        @pl.when((s == nt) & (lax.rem(b, 2) == p))
        def _():
            _wait_rows(ys_hbm, ybuf[p], sem.at[p])
            for cp in index_copies(s, 1 - p):
                cp.wait()
            finish(p)


def _combine(l, pos_rows, ys, x1, mod, final_norm_w, seq, tm):
    t, d = x1.shape
    nt = t // tm
    per_b = seq // tm
    tile = lambda s: jnp.maximum(s - 1, 0)
    return pl.pallas_call(
        _combine_kernel,
        grid=(nt + 1,),
        in_specs=[
            pl.BlockSpec(memory_space=pl.ANY),
            pl.BlockSpec(memory_space=pl.ANY),
            pl.BlockSpec((tm, d), lambda s: (tile(s), 0)),
            pl.BlockSpec((None, None, None, 1, d), lambda s: (l, tile(s) // per_b, 5, 0, 0)),
            pl.BlockSpec((1, d), lambda s: (0, 0)),
        ],
        out_specs=pl.BlockSpec((tm, d), lambda s: (tile(s), 0)),
        out_shape=jax.ShapeDtypeStruct((t, d), F32),
        scratch_shapes=[
            pltpu.SMEM((tm,), jnp.int32),
            pltpu.SMEM((tm,), jnp.int32),
            pltpu.VMEM((tm, d), F32),
            pltpu.VMEM((tm, d), F32),
            pltpu.SemaphoreType.DMA((2,)),
            pltpu.SemaphoreType.DMA(()),
        ],
        compiler_params=_cparams(("arbitrary",)),
        name="combine",
    )(pos_rows, ys, x1, mod, final_norm_w.reshape(1, d).astype(F32))


def _routing_tables(bucket, rank, counts, tm_e):
    nq = counts.shape[0]
    t = bucket.shape[0]
    nb = t // tm_e + nq
    buckets = jnp.arange(nq, dtype=jnp.int32)
    padded = jnp.maximum((counts + tm_e - 1) // tm_e, 1) * tm_e
    pends = jnp.cumsum(padded)
    pstarts = pends - padded
    starts = jnp.cumsum(counts) - counts
    pos = jnp.sum(jnp.where(bucket[None, :] == buckets[:, None], pstarts[:, None], 0), axis=0) + rank
    n_used = (pends[-1] // tm_e).astype(jnp.int32).reshape(1)
    first_slot = jnp.arange(nb, dtype=jnp.int32) * tm_e
    block_bucket = jnp.minimum(jnp.sum(pends[None, :] <= first_slot[:, None], axis=1), nq - 1).astype(jnp.int32)
    onehot = block_bucket[:, None] == buckets[None, :]
    block_first = (jnp.sum(jnp.where(onehot, pstarts[None, :], 0), axis=1) == first_slot).astype(jnp.int32)
    shift = jnp.sum(jnp.where(onehot, (pstarts - starts)[None, :], 0), axis=1)
    block_src = jnp.clip(first_slot - shift, 0, t - 1).astype(jnp.int32)
    keys = jnp.sort(bucket * t + jnp.arange(t, dtype=jnp.int32))
    tok_sorted = keys % t
    n_rows = t // LANES + tm_e // LANES + 1
    tok_rows = jnp.zeros((n_rows * LANES,), jnp.int32).at[:t].set(tok_sorted).reshape(n_rows, LANES)
    return block_bucket, block_first, block_src, n_used, tok_rows, pos.reshape(t // LANES, LANES)


def kernel(x, c, w_mod, b_mod, norm1_w, w_in, conv_dw_w, conv_dw_b, conv_ln_w, conv_ln_b,
           ssd_conv_w, ssd_conv_b, a_log, dt_bias, d_skip, ssd_norm_w, w_out, norm2_w, w_router,
           router_bias, w_gate, w_up, w_down, final_norm_w):
    bsz, seq, d = x.shape
    depth = w_mod.shape[0]
    t = bsz * seq
    d_conv = conv_dw_w.shape[2]
    d_ssd = ssd_norm_w.shape[1]
    xbc_w = ssd_conv_w.shape[2]
    heads = a_log.shape[1]
    n_main = w_in.shape[2] - heads
    tm_e = min(TM_EXPERT, seq)
    tm_c = min(TM_COMBINE, seq)
    assert seq % SSD_CHUNK == 0 and n_main % TN_PROJ == 0 and d_conv % CONV_GROUP == 0
    assert 2 * d_conv % d_ssd == 0 and (2 * d_conv + d_ssd) % xbc_w == 0
    assert tm_e % LANES == 0 and tm_c % LANES == 0 and heads <= LANES

    row3 = lambda v: v.reshape(depth, 1, -1).astype(F32)
    lane_pad = lambda v: jnp.zeros((depth, 1, LANES), F32).at[:, 0, :heads].set(v.astype(F32))
    w_in_bf = w_in[:, :, :n_main].astype(BF16)
    w_dt = jnp.zeros((depth, d, LANES), BF16).at[:, :, :heads].set(w_in[:, :, n_main:].astype(BF16))
    w_out_bf = w_out.astype(BF16)
    wr_hi, wr_lo = _split2(w_router.T)
    head_of_ch = jnp.arange(d_ssd, dtype=jnp.int32) // SSD_HEAD_DIM
    expand = (jnp.arange(LANES, dtype=jnp.int32)[:, None] == head_of_ch[None, :]).astype(BF16)
    dskip_e = jnp.repeat(d_skip.astype(F32), SSD_HEAD_DIM, axis=1).reshape(depth, 1, d_ssd)

    mod = _modulation(c, w_mod, b_mod).reshape(depth, bsz, 6, 1, d)
    x2d = x.reshape(t, d)
    for l in range(depth):
        if l == 0:
            proj, dt_raw = _in_proj(l, x2d, mod, row3(norm1_w), w_in_bf, w_dt, n_main, seq)
        else:
            x2d, proj, dt_raw = _in_proj_combine(l, pos_rows, ys, x1, mod, row3(norm1_w), w_in_bf, w_dt,
                                                 n_main, seq)
        y_conv = _conformer_conv(l, proj, conv_dw_w, row3(conv_dw_b), row3(conv_ln_w), row3(conv_ln_b),
                                 bsz, seq)
        y_ssd = _ssd(l, proj, dt_raw, ssd_conv_w, row3(ssd_conv_b), lane_pad(a_log), lane_pad(dt_bias),
                     dskip_e, row3(ssd_norm_w), expand, bsz, seq, 2 * d_conv)
        x1, h2x, qr, cnt = _out_proj(l, y_conv, y_ssd, x2d, mod, row3(norm2_w), w_out_bf,
                                     wr_hi, wr_lo, router_bias, seq)
        block_bucket, block_first, block_src, n_used, tok_rows, pos_rows = _routing_tables(
            qr[0], qr[1], cnt[:N_BUCKETS, 0], tm_e)
        ys = _experts(l, h2x, block_bucket, block_first, block_src, n_used, tok_rows, w_gate, w_up, w_down, tm_e)
    out = _combine(depth - 1, pos_rows, ys, x1, mod, final_norm_w, seq, tm_c)
    return out.reshape(bsz, seq, d)
```

```python
import functools

import jax
import jax.numpy as jnp
from jax import lax
from jax.experimental import pallas as pl
from jax.experimental.pallas import tpu as pltpu

F32 = jnp.float32
BF16 = jnp.bfloat16
EPS = 1e-6

LANES = 128
LANE_SHIFT = LANES.bit_length() - 1
SUBLANES = 8
VMEM_LIMIT_BYTES = 56 * 1024 * 1024

CONV_GROUP = 128
CONV_HALO = 32
SSD_HEAD_DIM = 64
SSD_GROUPS = 2
SSD_STATE = 128
SSD_CHUNK = 128
SSD_HALO = 8
N_EXPERT_GROUPS = 8
EXPERTS_PER_GROUP = 4
PAIRS = ((0, 1), (0, 2), (0, 3), (1, 2), (1, 3), (2, 3))
PAIRS_PER_GROUP = len(PAIRS)
N_BUCKETS = N_EXPERT_GROUPS * PAIRS_PER_GROUP
BUCKET_ROWS = 64

TM_PROJ = 512
TL_CONV = 256
CONV_ROWS = 64
TM_EXPERT = 128
TM_COMBINE = 512
TN_MOD = 1024
TN_PROJ = 512


def _cparams(sem):
    return pltpu.CompilerParams(dimension_semantics=sem, vmem_limit_bytes=VMEM_LIMIT_BYTES)


def _sigmoid(v):
    return jax.nn.sigmoid(v)


def _split2(v):
    hi = v.astype(BF16)
    lo = (v - hi.astype(F32)).astype(BF16)
    return hi, lo


def _split3(v):
    p1 = v.astype(BF16)
    r1 = v - p1.astype(F32)
    p2 = r1.astype(BF16)
    p3 = (r1 - p2.astype(F32)).astype(BF16)
    return p1, p2, p3


def _dot(a, b):
    return jnp.dot(a, b, preferred_element_type=F32)


def _dot_nt(a, b):
    return lax.dot_general(a, b, (((1,), (1,)), ((), ())), preferred_element_type=F32)


def _mod_kernel(c_ref, w_ref, b_ref, o_ref):
    c = c_ref[...]
    nb = c.shape[0]
    ca = c * _sigmoid(c)
    c_hi, c_lo = _split2(ca)
    w_hi, w_lo = _split2(w_ref[...])
    r = _dot(jnp.concatenate([c_hi, c_lo], axis=0), w_hi)
    o_ref[...] = r[:nb] + r[nb:] + _dot(c_hi, w_lo) + b_ref[...]


def _modulation(c, w_mod, b_mod):
    depth, d, n = w_mod.shape
    bsz = c.shape[0]
    tn = min(TN_MOD, n)
    return pl.pallas_call(
        _mod_kernel,
        grid=(depth, n // tn),
        in_specs=[
            pl.BlockSpec((bsz, d), lambda l, j: (0, 0)),
            pl.BlockSpec((None, d, tn), lambda l, j: (l, 0, j)),
            pl.BlockSpec((None, 1, tn), lambda l, j: (l, 0, j)),
        ],
        out_specs=pl.BlockSpec((None, bsz, tn), lambda l, j: (l, 0, j)),
        out_shape=jax.ShapeDtypeStruct((depth, bsz, n), F32),
        compiler_params=_cparams(("arbitrary", "arbitrary")),
        name="modulation",
    )(c, w_mod, b_mod.reshape(depth, 1, n))


def _mod_spec(l, j, per_b, d):
    return pl.BlockSpec((None, None, None, 1, d), lambda i: (l, i // per_b, j, 0, 0))


def _rms_modulate(x, nw, sc, sh):
    ms = jnp.mean(x * x, axis=-1, keepdims=True)
    return (x * lax.rsqrt(ms + EPS) * nw) * (1.0 + sc) + sh


def _inproj_kernel(x_ref, sh_ref, sc_ref, nw_ref, w_ref, wdt_ref, o_ref, odt_ref):
    hb = _rms_modulate(x_ref[...], nw_ref[...], sc_ref[...], sh_ref[...]).astype(BF16)
    n_main = o_ref.shape[1]
    for c0 in range(0, n_main, TN_PROJ):
        o_ref[:, c0:c0 + TN_PROJ] = _dot(hb, w_ref[:, c0:c0 + TN_PROJ]).astype(BF16)
    odt_ref[...] = _dot(hb, wdt_ref[...])


def _in_proj(l, x2d, mod, norm_w, w_in, w_dt, n_main, seq):
    t, d = x2d.shape
    tm = min(TM_PROJ, seq)
    per_b = seq // tm
    return pl.pallas_call(
        _inproj_kernel,
        grid=(t // tm,),
        in_specs=[
            pl.BlockSpec((tm, d), lambda i: (i, 0)),
            _mod_spec(l, 0, per_b, d),
            _mod_spec(l, 1, per_b, d),
            pl.BlockSpec((None, 1, d), lambda i: (l, 0, 0)),
            pl.BlockSpec((None, d, n_main), lambda i: (l, 0, 0), pipeline_mode=pl.Buffered(1)),
            pl.BlockSpec((None, d, LANES), lambda i: (l, 0, 0), pipeline_mode=pl.Buffered(1)),
        ],
        out_specs=[
            pl.BlockSpec((tm, n_main), lambda i: (i, 0)),
            pl.BlockSpec((tm, LANES), lambda i: (i, 0)),
        ],
        out_shape=[
            jax.ShapeDtypeStruct((t, n_main), BF16),
            jax.ShapeDtypeStruct((t, LANES), F32),
        ],
        compiler_params=_cparams(("arbitrary",)),
        name="in_proj",
    )(x2d, mod, mod, norm_w, w_in, w_dt)


def _inproj_combine_kernel(pos_hbm, ys_hbm, x1_ref, g2_ref, sh_ref, sc_ref, nw_ref, w_ref, wdt_ref,
                           x2_ref, o_ref, odt_ref, idx_smem, ybuf, sem, isem):
    i = pl.program_id(0)
    last = pl.num_programs(0) - 1
    tm, d = x1_ref.shape
    rows_per_tile = tm // LANES

    def gather(tile):
        _gather_rows(pos_hbm, tile * rows_per_tile, 0, ys_hbm, idx_smem, isem, ybuf, sem, _token_rows(d))

    @pl.when(i == 0)
    def _():
        gather(0)

    _wait_rows(ys_hbm, ybuf, sem)
    x2 = x1_ref[...] + g2_ref[...] * _get_tokens(ybuf, tm, d)[0]
    x2_ref[...] = x2
    gather(jnp.minimum(i + 1, last))
    hb = _rms_modulate(x2, nw_ref[...], sc_ref[...], sh_ref[...]).astype(BF16)
    n_main = o_ref.shape[1]
    for c0 in range(0, n_main, TN_PROJ):
        o_ref[:, c0:c0 + TN_PROJ] = _dot(hb, w_ref[:, c0:c0 + TN_PROJ]).astype(BF16)
    odt_ref[...] = _dot(hb, wdt_ref[...])

    @pl.when(i == last)
    def _():
        _wait_rows(ys_hbm, ybuf, sem)


def _in_proj_combine(l, pos_rows, ys, x1, mod, norm_w, w_in, w_dt, n_main, seq):
    t, d = x1.shape
    tm = min(TM_PROJ, seq)
    per_b = seq // tm
    hbm = pl.BlockSpec(memory_space=pl.ANY)
    return pl.pallas_call(
        _inproj_combine_kernel,
        grid=(t // tm,),
        in_specs=[
            hbm, hbm,
            pl.BlockSpec((tm, d), lambda i: (i, 0)),
            _mod_spec(l - 1, 5, per_b, d),
            _mod_spec(l, 0, per_b, d),
            _mod_spec(l, 1, per_b, d),
            pl.BlockSpec((None, 1, d), lambda i: (l, 0, 0)),
            pl.BlockSpec((None, d, n_main), lambda i: (l, 0, 0), pipeline_mode=pl.Buffered(1)),
            pl.BlockSpec((None, d, LANES), lambda i: (l, 0, 0), pipeline_mode=pl.Buffered(1)),
        ],
        out_specs=[
            pl.BlockSpec((tm, d), lambda i: (i, 0)),
            pl.BlockSpec((tm, n_main), lambda i: (i, 0)),
            pl.BlockSpec((tm, LANES), lambda i: (i, 0)),
        ],
        out_shape=[
            jax.ShapeDtypeStruct((t, d), F32),
            jax.ShapeDtypeStruct((t, n_main), BF16),
            jax.ShapeDtypeStruct((t, LANES), F32),
        ],
        scratch_shapes=[
            pltpu.SMEM((tm,), jnp.int32),
            pltpu.VMEM((tm * _token_rows(d), LANES), F32),
            pltpu.SemaphoreType.DMA(()),
            pltpu.SemaphoreType.DMA(()),
        ],
        compiler_params=_cparams(("arbitrary",)),
        name="in_proj_combine",
    )(pos_rows, ys, x1, mod, mod, mod, norm_w, w_in, w_dt)


def _conv_kernel(a_ref, g_ref, w_ref, b_ref, lnw_ref, lnb_ref, o_ref, ubuf):
    tl, ch = o_ref.shape
    taps = w_ref.shape[0]
    n = CONV_HALO + tl
    i = pl.program_id(1)

    @pl.when(i == 0)
    def _():
        ubuf[0, 0:CONV_HALO, :] = jnp.zeros((CONV_HALO, ch), F32)

    @pl.when(i > 0)
    def _():
        ubuf[0, 0:CONV_HALO, :] = ubuf[0, tl:tl + CONV_HALO, :]

    ubuf[0, CONV_HALO:n, :] = a_ref[...].astype(F32) * _sigmoid(g_ref[...].astype(F32))
    for s in range(1, SUBLANES):
        ubuf[s, 0:n - s, :] = ubuf[0, s:n, :]

    rows = min(CONV_ROWS, tl)
    for cb in range(ch // CONV_GROUP):
        cs = slice(cb * CONV_GROUP, (cb + 1) * CONV_GROUP)
        bias = b_ref[:, cs]
        lnw = lnw_ref[:, cs]
        lnb = lnb_ref[:, cs]
        for rs in range(tl // rows):
            r0 = CONV_HALO - (taps - 1) + rs * rows
            acc = jnp.zeros((rows, CONV_GROUP), F32)
            for k in range(taps):
                phase = (r0 + k) % SUBLANES
                a0 = r0 + k - phase
                acc = acc + w_ref[k:k + 1, cs] * ubuf[phase, a0:a0 + rows, cs]
            acc = acc + bias
            mu = jnp.mean(acc, axis=-1, keepdims=True)
            dev = acc - mu
            var = jnp.mean(dev * dev, axis=-1, keepdims=True)
            yn = dev * lax.rsqrt(var + EPS) * lnw + lnb
            o_ref[rs * rows:(rs + 1) * rows, cs] = (yn * _sigmoid(yn)).astype(o_ref.dtype)


def _conformer_conv(l, proj, dw_w, dw_b, ln_w, ln_b, bsz, seq):
    _, taps, ch = dw_w.shape
    tl = min(TL_CONV, seq)
    nl = seq // tl
    row = lambda b, i: b * nl + i
    vec = pl.BlockSpec((None, 1, ch), lambda b, i: (l, 0, 0))
    return pl.pallas_call(
        _conv_kernel,
        grid=(bsz, nl),
        in_specs=[
            pl.BlockSpec((tl, ch), lambda b, i: (row(b, i), 0)),
            pl.BlockSpec((tl, ch), lambda b, i: (row(b, i), 1)),
            pl.BlockSpec((None, taps, ch), lambda b, i: (l, 0, 0)),
            vec, vec, vec,
        ],
        out_specs=pl.BlockSpec((tl, ch), lambda b, i: (row(b, i), 0)),
        out_shape=jax.ShapeDtypeStruct((bsz * seq, ch), BF16),
        scratch_shapes=[pltpu.VMEM((SUBLANES, CONV_HALO + tl, ch), F32)],
        compiler_params=_cparams(("arbitrary", "arbitrary")),
        name="conformer_conv",
    )(proj, proj, dw_w, dw_b, ln_w, ln_b)


def _ssd_kernel(z_ref, xbc_ref, dt_ref, cw_ref, cb_ref, alog_ref, dtb_ref, dskip_ref, nw_ref,
                expand_ref, o_ref, xbuf, state):
    q, d_ssd = o_ref.shape
    gw = d_ssd // SSD_GROUPS
    gn = SSD_GROUPS * SSD_STATE
    taps = cw_ref.shape[0]
    c = pl.program_id(1)

    @pl.when(c == 0)
    def _():
        xbuf[0:SSD_HALO, :] = jnp.zeros((SSD_HALO, xbuf.shape[1]), F32)
        state[...] = jnp.zeros(state.shape, F32)

    @pl.when(c > 0)
    def _():
        xbuf[0:SSD_HALO, :] = xbuf[q:q + SSD_HALO, :]

    xbuf[SSD_HALO:SSD_HALO + q, :] = xbc_ref[...].astype(F32)
    acc = jnp.zeros((q, xbuf.shape[1]), F32) + cb_ref[...]
    for k in range(taps):
        r0 = SSD_HALO - (taps - 1) + k
        acc = acc + cw_ref[k:k + 1, :] * xbuf[r0:r0 + q, :]
    xc = acc * _sigmoid(acc)
    xs = xc[:, :d_ssd]
    bm = xc[:, d_ssd:d_ssd + gn]
    cm = xc[:, d_ssd + gn:]

    dt = jax.nn.softplus(dt_ref[...] + dtb_ref[...])
    a = -jnp.exp(alog_ref[...])
    da = dt * a
    rows = lax.broadcasted_iota(jnp.int32, (q, q), 0)
    cols = lax.broadcasted_iota(jnp.int32, (q, q), 1)
    tril = rows >= cols
    tri = jnp.where(tril, 1.0, 0.0).astype(BF16)
    d1, d2, d3 = _split3(da)
    cs = _dot(tri, d1) + _dot(tri, d2) + _dot(tri, d3)
    cs_t = cs.T
    cs_last = cs[q - 1:q, :]

    expand = expand_ref[...]

    def per_channel(v):
        hi, lo = _split2(v)
        return _dot(hi, expand) + _dot(lo, expand)

    dt_e = per_channel(dt)
    ecs_e = per_channel(jnp.exp(cs))
    wds_e = per_channel(dt * jnp.exp(cs_last - cs))
    xd = (xs * dt_e).astype(BF16)
    xds = (xs * wds_e).astype(BF16)
    chunk_decay = ecs_e[q - 1:q, :]

    lane = lax.broadcasted_iota(jnp.int32, (q, LANES), 1)
    heads_per_group = gw // SSD_HEAD_DIM
    y_parts = []
    for g in range(SSD_GROUPS):
        bg = bm[:, g * SSD_STATE:(g + 1) * SSD_STATE]
        cg = cm[:, g * SSD_STATE:(g + 1) * SSD_STATE].astype(BF16)
        cb_mat = _dot_nt(cg, bg.astype(BF16))
        st = state[g]
        y_off = _dot(cg, st.astype(BF16)) * ecs_e[:, g * gw:(g + 1) * gw]
        diag_parts = []
        for pr in range(heads_per_group // 2):
            lo_col = g * gw + pr * LANES
            xd_blk = xd[:, lo_col:lo_col + LANES]
            yp = jnp.zeros((q, LANES), F32)
            for half in range(2):
                h = g * heads_per_group + 2 * pr + half
                diff = cs[:, h:h + 1] - cs_t[h:h + 1, :]
                decay = jnp.exp(jnp.where(tril, diff, -jnp.inf))
                m = (cb_mat * decay).astype(BF16)
                keep = (lane >= SSD_HEAD_DIM) if half else (lane < SSD_HEAD_DIM)
                yp = yp + _dot(m, jnp.where(keep, xd_blk, jnp.zeros_like(xd_blk)))
            diag_parts.append(yp)
        y_parts.append(jnp.concatenate(diag_parts, axis=1) + y_off)
        new = _dot(bg.T.astype(BF16), xds[:, g * gw:(g + 1) * gw])
        state[g] = st * chunk_decay[:, g * gw:(g + 1) * gw] + new

    y = jnp.concatenate(y_parts, axis=1) + dskip_ref[...] * xs
    z = z_ref[...].astype(F32)
    y = y * (z * _sigmoid(z))
    outs = []
    for g in range(SSD_GROUPS):
        yg = y[:, g * gw:(g + 1) * gw]
        outs.append(yg * lax.rsqrt(jnp.mean(yg * yg, axis=-1, keepdims=True) + EPS))
    o_ref[...] = (jnp.concatenate(outs, axis=1) * nw_ref[...]).astype(o_ref.dtype)


def _ssd(l, proj, dt_raw, conv_w, conv_b, a_log, dt_bias, dskip_e, norm_w, expand, bsz, seq, z_col0):
    _, taps, xbc_w = conv_w.shape
    d_ssd = norm_w.shape[2]
    q = SSD_CHUNK
    nc = seq // q
    gw = d_ssd // SSD_GROUPS
    row = lambda b, c: b * nc + c
    vec = lambda n: pl.BlockSpec((None, 1, n), lambda b, c: (l, 0, 0))
    return pl.pallas_call(
        _ssd_kernel,
        grid=(bsz, nc),
        in_specs=[
            pl.BlockSpec((q, d_ssd), lambda b, c: (row(b, c), z_col0 // d_ssd)),
            pl.BlockSpec((q, xbc_w), lambda b, c: (row(b, c), (z_col0 + d_ssd) // xbc_w)),
            pl.BlockSpec((q, LANES), lambda b, c: (row(b, c), 0)),
            pl.BlockSpec((None, taps, xbc_w), lambda b, c: (l, 0, 0)),
            vec(xbc_w), vec(LANES), vec(LANES), vec(d_ssd), vec(d_ssd),
            pl.BlockSpec((LANES, d_ssd), lambda b, c: (0, 0)),
        ],
        out_specs=pl.BlockSpec((q, d_ssd), lambda b, c: (row(b, c), 0)),
        out_shape=jax.ShapeDtypeStruct((bsz * seq, d_ssd), BF16),
        scratch_shapes=[
            pltpu.VMEM((SSD_HALO + q, xbc_w), F32),
            pltpu.VMEM((SSD_GROUPS, SSD_STATE, gw), F32),
        ],
        compiler_params=_cparams(("arbitrary", "arbitrary")),
        name="ssd",
    )(proj, proj, dt_raw, conv_w, conv_b, a_log, dt_bias, dskip_e, norm_w, expand)


def _outproj_kernel(yc_ref, ys_ref, x_ref, g1_ref, sh2_ref, sc2_ref, nw_ref, w_ref, wr_hi_ref,
                    wr_lo_ref, rb_ref, x1_ref, h2_ref, qr_ref, cnt_ref, base):
    tm, d = x_ref.shape
    n_exp = wr_hi_ref.shape[0]
    dc = yc_ref.shape[1]
    i = pl.program_id(0)

    @pl.when(i == 0)
    def _():
        base[...] = jnp.zeros(base.shape, F32)

    mix = _dot(yc_ref[...], w_ref[0:dc, :]) + _dot(ys_ref[...], w_ref[dc:, :])
    x1 = x_ref[...] + g1_ref[...] * mix
    x1_ref[...] = x1
    h2 = _rms_modulate(x1, nw_ref[...], sc2_ref[...], sh2_ref[...])

    h_hi, h_lo = _split2(h2)
    wr_hi = wr_hi_ref[...]
    logits = _dot_nt(wr_hi, h_hi) + _dot_nt(wr_lo_ref[...], h_hi) + _dot_nt(wr_hi, h_lo)
    s = _sigmoid(logits)
    s_sel = s + rb_ref[...]
    eiota = lax.broadcasted_iota(jnp.int32, s.shape, 0)
    group_of = lax.shift_right_logical(eiota, EXPERTS_PER_GROUP.bit_length() - 1)
    in_group = jnp.bitwise_and(eiota, EXPERTS_PER_GROUP - 1)
    pair_max = jnp.full(s.shape, -jnp.inf, F32)
    for j in range(1, EXPERTS_PER_GROUP):
        partner = pltpu.roll(s_sel, n_exp - j, 0)
        pair_max = jnp.maximum(pair_max, jnp.where(in_group < EXPERTS_PER_GROUP - j,
                                                   s_sel + partner, -jnp.inf))
    gmax = jnp.max(pair_max, axis=0, keepdims=True)
    best = jnp.min(jnp.where(pair_max == gmax, group_of, N_EXPERT_GROUPS), axis=0, keepdims=True)
    masked = jnp.where(group_of == best, s_sel, -jnp.inf)
    m1 = jnp.max(masked, axis=0, keepdims=True)
    i1 = jnp.min(jnp.where(masked == m1, eiota, n_exp), axis=0, keepdims=True)
    masked = jnp.where(eiota == i1, -jnp.inf, masked)
    m2 = jnp.max(masked, axis=0, keepdims=True)
    i2 = jnp.min(jnp.where(masked == m2, eiota, n_exp), axis=0, keepdims=True)
    e_lo = jnp.minimum(i1, i2)
    e_hi = jnp.maximum(i1, i2)
    a_lo = jnp.sum(jnp.where(eiota == e_lo, s, 0.0), axis=0, keepdims=True)
    a_hi = jnp.sum(jnp.where(eiota == e_hi, s, 0.0), axis=0, keepdims=True)
    tot = a_lo + a_hi
    liota = lax.broadcasted_iota(jnp.int32, (LANES, tm), 0)
    gates = jnp.where(liota == 0, a_lo / tot, jnp.where(liota == 1, a_hi / tot, 0.0))
    _put_tokens(h2_ref, h2, gates.T)
    in_lo = jnp.bitwise_and(e_lo, EXPERTS_PER_GROUP - 1)
    in_hi = jnp.bitwise_and(e_hi, EXPERTS_PER_GROUP - 1)
    bucket = (lax.shift_right_logical(e_lo, EXPERTS_PER_GROUP.bit_length() - 1) * PAIRS_PER_GROUP
              + _pair_index(in_lo, in_hi))
    qr_ref[0:1, :] = bucket

    oh = jnp.where(lax.broadcasted_iota(jnp.int32, (base.shape[0], tm), 0) == bucket, 1.0, 0.0)
    r_i = lax.broadcasted_iota(jnp.int32, (tm, tm), 0)
    c_i = lax.broadcasted_iota(jnp.int32, (tm, tm), 1)
    upper = jnp.where(r_i <= c_i, 1.0, 0.0).astype(BF16)
    incl = _dot(oh.astype(BF16), upper)
    rank_all = base[...] + incl - oh
    qr_ref[1:2, :] = jnp.sum(oh * rank_all, axis=0, keepdims=True).astype(jnp.int32)
    new_base = base[...] + jnp.sum(oh, axis=1, keepdims=True)
    base[...] = new_base
    cnt_ref[...] = jnp.broadcast_to(new_base, cnt_ref.shape).astype(jnp.int32)


def _out_proj(l, y_conv, y_ssd, x2d, mod, norm_w, w_out, wr_hi, wr_lo, router_bias, seq):
    t, d = x2d.shape
    dc = y_conv.shape[1]
    ds = y_ssd.shape[1]
    n_exp = wr_hi.shape[0]
    tm = min(TM_PROJ, seq)
    per_b = seq // tm
    const = lambda shape: pl.BlockSpec(shape, lambda i: (0, 0))
    return pl.pallas_call(
        _outproj_kernel,
        grid=(t // tm,),
        in_specs=[
            pl.BlockSpec((tm, dc), lambda i: (i, 0)),
            pl.BlockSpec((tm, ds), lambda i: (i, 0)),
            pl.BlockSpec((tm, d), lambda i: (i, 0)),
            _mod_spec(l, 2, per_b, d), _mod_spec(l, 3, per_b, d), _mod_spec(l, 4, per_b, d),
            pl.BlockSpec((None, 1, d), lambda i: (l, 0, 0)),
            pl.BlockSpec((None, dc + ds, d), lambda i: (l, 0, 0), pipeline_mode=pl.Buffered(1)),
            const((n_exp, d)), const((n_exp, d)), const((n_exp, 1)),
        ],
        out_specs=[
            pl.BlockSpec((tm, d), lambda i: (i, 0)),
            pl.BlockSpec((tm * _token_rows(d), LANES), lambda i: (i, 0)),
            pl.BlockSpec((2, tm), lambda i: (0, i)),
            const((BUCKET_ROWS, LANES)),
        ],
        out_shape=[
            jax.ShapeDtypeStruct((t, d), F32),
            jax.ShapeDtypeStruct((t * _token_rows(d), LANES), F32),
            jax.ShapeDtypeStruct((2, t), jnp.int32),
            jax.ShapeDtypeStruct((BUCKET_ROWS, LANES), jnp.int32),
        ],
        scratch_shapes=[pltpu.VMEM((BUCKET_ROWS, 1), F32)],
        compiler_params=_cparams(("arbitrary",)),
        name="out_proj_router",
    )(y_conv, y_ssd, x2d, mod, mod, mod, norm_w, w_out, wr_hi, wr_lo,
      router_bias.reshape(n_exp, 1).astype(F32))


def _index_copies(idx_hbm, row0, idx_smem, isem):
    return [pltpu.make_async_copy(idx_hbm.at[row0 + i], idx_smem.at[pl.ds(i * LANES, LANES)], isem)
            for i in range(idx_smem.shape[0] // LANES)]


def _token_rows(d):
    return d // LANES + 1


def _put_tokens(ref, values, extra):
    tm, d = values.shape
    span = _token_rows(d)
    for cc in range(span - 1):
        ref[pl.ds(cc, tm, stride=span), :] = values[:, cc * LANES:(cc + 1) * LANES]
    ref[pl.ds(span - 1, tm, stride=span), :] = extra


def _get_tokens(ref, tm, d):
    span = _token_rows(d)
    chunks = [ref[pl.ds(cc, tm, stride=span), :] for cc in range(span)]
    return jnp.concatenate(chunks[:-1], axis=1), chunks[-1]


def _start_rows(idx_smem, delta, src_hbm, buf, sem, span=1):
    for r in range(buf.shape[0] // span):
        pltpu.make_async_copy(src_hbm.at[pl.ds(idx_smem[delta + r] * span, span)],
                              buf.at[pl.ds(r * span, span)], sem).start()


def _gather_rows(idx_hbm, row0, delta, src_hbm, idx_smem, isem, buf, sem, span=1):
    copies = _index_copies(idx_hbm, row0, idx_smem, isem)
    for cp in copies:
        cp.start()
    for cp in copies:
        cp.wait()
    _start_rows(idx_smem, delta, src_hbm, buf, sem, span)


def _wait_rows(src_hbm, buf, sem):
    pltpu.make_async_copy(src_hbm.at[pl.ds(0, buf.shape[0])], buf, sem).wait()


def _pair_index(lo, hi):
    idx = 0
    for p, (a, b) in enumerate(PAIRS):
        idx = jnp.where((lo == a) & (hi == b), p, idx)
    return idx


def _pair_members(p):
    lo = hi = 0
    for k, (a, b) in enumerate(PAIRS):
        lo = jnp.where(p == k, a, lo)
        hi = jnp.where(p == k, b, hi)
    return lo, hi


WEIGHT_EVENTS = {0: (2, 3), 1: (3, 4), 3: (0, 5), 5: (1, 6)}


def _expert_kernel(bq_ref, first_ref, src_ref, nused_ref, tok_hbm, h_hbm, wg_hbm, wu_hbm, wd_hbm, o_ref,
                   idx0, idx1, idx2, xbuf0, xbuf1, xbuf2, sem, isem, wsem, sg, su, sd, wg_bf, wu_bf, wd_bf,
                   *, layer):
    s = pl.program_id(0)
    b = s - 1
    n_used = nused_ref[0]
    n_exp = wg_hbm.shape[1]
    d = wg_hbm.shape[2]
    span = _token_rows(d)
    tm = o_ref.shape[0] // span
    idx = (idx0, idx1, idx2)
    xbuf = (xbuf0, xbuf1, xbuf2)
    nbuf = len(xbuf)
    stage = ((wg_hbm, sg, wg_bf), (wu_hbm, su, wu_bf), (wd_hbm, sd, wd_bf))
    b_tab = jnp.clip(b, 0, bq_ref.shape[0] - 1)
    group = bq_ref[b_tab] // PAIRS_PER_GROUP
    pair = lax.rem(bq_ref[b_tab], PAIRS_PER_GROUP)

    def index_copies(blk, p):
        src = src_ref[jnp.minimum(blk, n_used - 1)]
        return _index_copies(tok_hbm, lax.shift_right_logical(src, LANE_SHIFT), idx[p], isem)

    def start_rows(blk, p):
        src = src_ref[jnp.minimum(blk, n_used - 1)]
        _start_rows(idx[p], jnp.bitwise_and(src, LANES - 1), h_hbm, xbuf[p], sem.at[p], span)

    def weights_start(e):
        for j, (w_hbm, st, _) in enumerate(stage):
            pltpu.make_async_copy(w_hbm.at[layer, e], st, wsem.at[j]).start()

    def weights_finish(slot):
        for j, (w_hbm, st, w_bf) in enumerate(stage):
            pltpu.make_async_copy(w_hbm.at[layer, 0], st, wsem.at[j]).wait()
            w_bf[slot] = st[...].astype(BF16)

    def compute(p):
        lo, hi = _pair_members(pair)
        x, gates = _get_tokens(xbuf[p], tm, d)
        x = x.astype(BF16)
        out = None
        for slot, col in ((lo, 0), (hi, 1)):
            gate = _dot(x, wg_bf[slot])
            up = _dot(x, wu_bf[slot])
            hb = (gate * _sigmoid(gate) * up).astype(BF16)
            part = _dot(hb, wd_bf[slot]) * gates[:, col:col + 1]
            out = part if out is None else out + part
        _put_tokens(o_ref, out, jnp.zeros((tm, LANES), F32))

    @pl.when(s == 0)
    def _():
        for k in range(2):
            for cp in index_copies(k, k):
                cp.start()
                cp.wait()
            start_rows(k, k)
        for cp in index_copies(2, 2):
            cp.start()
        for e in range(2):
            weights_start(e)
            weights_finish(e)
        weights_start(2)

    is_first = (b >= 0) & (b < n_used) & (first_ref[b_tab] == 1)
    for ev_pair, (slot, nxt) in WEIGHT_EVENTS.items():
        e_next = group * EXPERTS_PER_GROUP + nxt

        @pl.when(is_first & (pair == ev_pair) & (group * EXPERTS_PER_GROUP + nxt - 1 < n_exp))
        def _():
            weights_finish(slot)

            @pl.when(e_next < n_exp)
            def _():
                weights_start(e_next)

    for p in range(nbuf):
        @pl.when((b >= 0) & (b < n_used) & (lax.rem(b, nbuf) == p))
        def _():
            for cp in index_copies(s + 1, (p + 2) % nbuf):
                cp.wait()
            start_rows(s + 1, (p + 2) % nbuf)
            for cp in index_copies(s + 2, p):
                cp.start()

        @pl.when((b >= 0) & (b < n_used) & (lax.rem(b, nbuf) == p))
        def _():
            _wait_rows(h_hbm, xbuf[p], sem.at[p])
            compute(p)

        @pl.when((b >= n_used) & (b < n_used + 2) & (lax.rem(b, nbuf) == p))
        def _():
            _wait_rows(h_hbm, xbuf[p], sem.at[p])

            @pl.when(b == n_used)
            def _():
                for cp in index_copies(s + 1, (p + 2) % nbuf):
                    cp.wait()

    @pl.when(b >= n_used)
    def _():
        o_ref[...] = jnp.zeros(o_ref.shape, o_ref.dtype)


def _experts(l, h2x, block_bucket, block_first, block_src, n_used, tok_rows, w_gate, w_up, w_down, tm):
    nb = block_bucket.shape[0]
    _, n_exp, d, de = w_gate.shape
    span = _token_rows(d)
    assert n_exp == N_EXPERT_GROUPS * EXPERTS_PER_GROUP and h2x.shape[1] == LANES and span % 2 == 1
    hbm = pl.BlockSpec(memory_space=pl.ANY)
    grid_spec = pltpu.PrefetchScalarGridSpec(
        num_scalar_prefetch=4,
        grid=(nb + 2,),
        in_specs=[hbm, hbm, hbm, hbm, hbm],
        out_specs=pl.BlockSpec((tm * span, LANES), lambda s, *_: (jnp.clip(s - 1, 0, nb - 1), 0)),
        scratch_shapes=[
            pltpu.SMEM((tm + LANES,), jnp.int32),
            pltpu.SMEM((tm + LANES,), jnp.int32),
            pltpu.SMEM((tm + LANES,), jnp.int32),
            pltpu.VMEM((tm * span, LANES), F32),
            pltpu.VMEM((tm * span, LANES), F32),
            pltpu.VMEM((tm * span, LANES), F32),
            pltpu.SemaphoreType.DMA((3,)),
            pltpu.SemaphoreType.DMA(()),
            pltpu.SemaphoreType.DMA((3,)),
            pltpu.VMEM((d, de), F32),
            pltpu.VMEM((d, de), F32),
            pltpu.VMEM((de, d), F32),
            pltpu.VMEM((EXPERTS_PER_GROUP, d, de), BF16),
            pltpu.VMEM((EXPERTS_PER_GROUP, d, de), BF16),
            pltpu.VMEM((EXPERTS_PER_GROUP, de, d), BF16),
        ],
    )
    return pl.pallas_call(
        functools.partial(_expert_kernel, layer=l),
        grid_spec=grid_spec,
        out_shape=jax.ShapeDtypeStruct((nb * tm * span, LANES), F32),
        compiler_params=_cparams(("arbitrary",)),
        name="experts",
    )(block_bucket, block_first, block_src, n_used, tok_rows, h2x, w_gate, w_up, w_down)


def _combine_kernel(pos_hbm, ys_hbm, x1_ref, g2_ref, fnw_ref, o_ref,
                    idx0, idx1, ybuf0, ybuf1, sem, isem):
    s = pl.program_id(0)
    nt = pl.num_programs(0) - 1
    b = s - 1
    tm, d = x1_ref.shape
    span = _token_rows(d)
    idx = (idx0, idx1)
    ybuf = (ybuf0, ybuf1)
    rows_per_tile = tm // LANES

    def index_copies(tile, p):
        return _index_copies(pos_hbm, jnp.minimum(tile, nt - 1) * rows_per_tile, idx[p], isem)

    def finish(p):
        x2 = x1_ref[...] + g2_ref[...] * _get_tokens(ybuf[p], tm, d)[0]
        ms = jnp.mean(x2 * x2, axis=-1, keepdims=True)
        o_ref[...] = x2 * lax.rsqrt(ms + EPS) * fnw_ref[...]

    @pl.when(s == 0)
    def _():
        for cp in index_copies(0, 0):
            cp.start()
            cp.wait()
        _start_rows(idx[0], 0, ys_hbm, ybuf[0], sem.at[0], span)
        for cp in index_copies(1, 1):
            cp.start()

    for p in range(2):
        @pl.when((b >= 0) & (s < nt) & (lax.rem(b, 2) == p))
        def _():
            _wait_rows(ys_hbm, ybuf[p], sem.at[p])
            for cp in index_copies(s, 1 - p):
                cp.wait()
            _start_rows(idx[1 - p], 0, ys_hbm, ybuf[1 - p], sem.at[1 - p], span)
            for cp in index_copies(s + 1, p):
                cp.start()
            finish(p)

        @pl.when((s == nt) & (lax.rem(b, 2) == p))
        def _():
            _wait_rows(ys_hbm, ybuf[p], sem.at[p])
            for cp in index_copies(s, 1 - p):
                cp.wait()
            finish(p)


def _combine(l, pos_rows, ys, x1, mod, final_norm_w, seq, tm):
    t, d = x1.shape
    nt = t // tm
    per_b = seq // tm
    tile = lambda s: jnp.maximum(s - 1, 0)
    return pl.pallas_call(
        _combine_kernel,
        grid=(nt + 1,),
        in_specs=[
            pl.BlockSpec(memory_space=pl.ANY),
            pl.BlockSpec(memory_space=pl.ANY),
            pl.BlockSpec((tm, d), lambda s: (tile(s), 0)),
            pl.BlockSpec((None, None, None, 1, d), lambda s: (l, tile(s) // per_b, 5, 0, 0)),
            pl.BlockSpec((1, d), lambda s: (0, 0)),
        ],
        out_specs=pl.BlockSpec((tm, d), lambda s: (tile(s), 0)),
        out_shape=jax.ShapeDtypeStruct((t, d), F32),
        scratch_shapes=[
            pltpu.SMEM((tm,), jnp.int32),
            pltpu.SMEM((tm,), jnp.int32),
            pltpu.VMEM((tm * _token_rows(d), LANES), F32),
            pltpu.VMEM((tm * _token_rows(d), LANES), F32),
            pltpu.SemaphoreType.DMA((2,)),
            pltpu.SemaphoreType.DMA(()),
        ],
        compiler_params=_cparams(("arbitrary",)),
        name="combine",
    )(pos_rows, ys, x1, mod, final_norm_w.reshape(1, d).astype(F32))


def _routing_tables(bucket, rank, counts, tm_e):
    nq = counts.shape[0]
    t = bucket.shape[0]
    nb = t // tm_e + nq
    buckets = jnp.arange(nq, dtype=jnp.int32)
    padded = jnp.maximum((counts + tm_e - 1) // tm_e, 1) * tm_e
    pends = jnp.cumsum(padded)
    pstarts = pends - padded
    starts = jnp.cumsum(counts) - counts
    pos = jnp.sum(jnp.where(bucket[None, :] == buckets[:, None], pstarts[:, None], 0), axis=0) + rank
    n_used = (pends[-1] // tm_e).astype(jnp.int32).reshape(1)
    first_slot = jnp.arange(nb, dtype=jnp.int32) * tm_e
    block_bucket = jnp.minimum(jnp.sum(pends[None, :] <= first_slot[:, None], axis=1), nq - 1).astype(jnp.int32)
    onehot = block_bucket[:, None] == buckets[None, :]
    block_first = (jnp.sum(jnp.where(onehot, pstarts[None, :], 0), axis=1) == first_slot).astype(jnp.int32)
    shift = jnp.sum(jnp.where(onehot, (pstarts - starts)[None, :], 0), axis=1)
    block_src = jnp.clip(first_slot - shift, 0, t - 1).astype(jnp.int32)
    keys = jnp.sort(bucket * t + jnp.arange(t, dtype=jnp.int32))
    tok_sorted = keys % t
    n_rows = t // LANES + tm_e // LANES + 1
    tok_rows = jnp.zeros((n_rows * LANES,), jnp.int32).at[:t].set(tok_sorted).reshape(n_rows, LANES)
    return block_bucket, block_first, block_src, n_used, tok_rows, pos.reshape(t // LANES, LANES)


def kernel(x, c, w_mod, b_mod, norm1_w, w_in, conv_dw_w, conv_dw_b, conv_ln_w, conv_ln_b,
           ssd_conv_w, ssd_conv_b, a_log, dt_bias, d_skip, ssd_norm_w, w_out, norm2_w, w_router,
           router_bias, w_gate, w_up, w_down, final_norm_w):
    bsz, seq, d = x.shape
    depth = w_mod.shape[0]
    t = bsz * seq
    d_conv = conv_dw_w.shape[2]
    d_ssd = ssd_norm_w.shape[1]
    xbc_w = ssd_conv_w.shape[2]
    heads = a_log.shape[1]
    n_main = w_in.shape[2] - heads
    tm_e = min(TM_EXPERT, seq)
    tm_c = min(TM_COMBINE, seq)
    assert seq % SSD_CHUNK == 0 and n_main % TN_PROJ == 0 and d_conv % CONV_GROUP == 0
    assert 2 * d_conv % d_ssd == 0 and (2 * d_conv + d_ssd) % xbc_w == 0
    assert tm_e % LANES == 0 and tm_c % LANES == 0 and heads <= LANES

    row3 = lambda v: v.reshape(depth, 1, -1).astype(F32)
    lane_pad = lambda v: jnp.zeros((depth, 1, LANES), F32).at[:, 0, :heads].set(v.astype(F32))
    w_in_bf = w_in[:, :, :n_main].astype(BF16)
    w_dt = jnp.zeros((depth, d, LANES), BF16).at[:, :, :heads].set(w_in[:, :, n_main:].astype(BF16))
    w_out_bf = w_out.astype(BF16)
    wr_hi, wr_lo = _split2(w_router.T)
    head_of_ch = jnp.arange(d_ssd, dtype=jnp.int32) // SSD_HEAD_DIM
    expand = (jnp.arange(LANES, dtype=jnp.int32)[:, None] == head_of_ch[None, :]).astype(BF16)
    dskip_e = jnp.repeat(d_skip.astype(F32), SSD_HEAD_DIM, axis=1).reshape(depth, 1, d_ssd)

    mod = _modulation(c, w_mod, b_mod).reshape(depth, bsz, 6, 1, d)
    x2d = x.reshape(t, d)
    for l in range(depth):
        if l == 0:
            proj, dt_raw = _in_proj(l, x2d, mod, row3(norm1_w), w_in_bf, w_dt, n_main, seq)
        else:
            x2d, proj, dt_raw = _in_proj_combine(l, pos_rows, ys, x1, mod, row3(norm1_w), w_in_bf, w_dt,
                                                 n_main, seq)
        y_conv = _conformer_conv(l, proj, conv_dw_w, row3(conv_dw_b), row3(conv_ln_w), row3(conv_ln_b),
                                 bsz, seq)
        y_ssd = _ssd(l, proj, dt_raw, ssd_conv_w, row3(ssd_conv_b), lane_pad(a_log), lane_pad(dt_bias),
                     dskip_e, row3(ssd_norm_w), expand, bsz, seq, 2 * d_conv)
        x1, h2x, qr, cnt = _out_proj(l, y_conv, y_ssd, x2d, mod, row3(norm2_w), w_out_bf,
                                     wr_hi, wr_lo, router_bias, seq)
        block_bucket, block_first, block_src, n_used, tok_rows, pos_rows = _routing_tables(
            qr[0], qr[1], cnt[:N_BUCKETS, 0], tm_e)
        ys = _experts(l, h2x, block_bucket, block_first, block_src, n_used, tok_rows, w_gate, w_up, w_down, tm_e)
    out = _combine(depth - 1, pos_rows, ys, x1, mod, final_norm_w, seq, tm_c)
    return out.reshape(bsz, seq, d)
```

```python
import functools

import jax
import jax.numpy as jnp
from jax import lax
from jax.experimental import pallas as pl
from jax.experimental.pallas import tpu as pltpu

F32 = jnp.float32
BF16 = jnp.bfloat16
EPS = 1e-6

LANES = 128
LANE_SHIFT = LANES.bit_length() - 1
SUBLANES = 8
VMEM_LIMIT_BYTES = 56 * 1024 * 1024

CONV_GROUP = 128
CONV_HALO = 32
SSD_HEAD_DIM = 64
SSD_GROUPS = 2
SSD_STATE = 128
SSD_CHUNK = 128
SSD_HALO = 8
N_EXPERT_GROUPS = 8
EXPERTS_PER_GROUP = 4
PAIRS = ((0, 1), (0, 2), (0, 3), (1, 2), (1, 3), (2, 3))
PAIRS_PER_GROUP = len(PAIRS)
N_BUCKETS = N_EXPERT_GROUPS * PAIRS_PER_GROUP
BUCKET_ROWS = 64

TM_PROJ = 512
TL_CONV = 512
CONV_ROWS = 64
TM_EXPERT = 128
TM_COMBINE = 512
TN_MOD = 1024
TN_PROJ = 512


def _cparams(sem):
    return pltpu.CompilerParams(dimension_semantics=sem, vmem_limit_bytes=VMEM_LIMIT_BYTES)


def _sigmoid(v):
    return jax.nn.sigmoid(v)


def _split2(v):
    hi = v.astype(BF16)
    lo = (v - hi.astype(F32)).astype(BF16)
    return hi, lo


def _split3(v):
    p1 = v.astype(BF16)
    r1 = v - p1.astype(F32)
    p2 = r1.astype(BF16)
    p3 = (r1 - p2.astype(F32)).astype(BF16)
    return p1, p2, p3


def _dot(a, b):
    return jnp.dot(a, b, preferred_element_type=F32)


def _dot_nt(a, b):
    return lax.dot_general(a, b, (((1,), (1,)), ((), ())), preferred_element_type=F32)


def _mod_kernel(c_ref, w_ref, b_ref, o_ref):
    c = c_ref[...]
    nb = c.shape[0]
    ca = c * _sigmoid(c)
    c_hi, c_lo = _split2(ca)
    w_hi, w_lo = _split2(w_ref[...])
    r = _dot(jnp.concatenate([c_hi, c_lo], axis=0), w_hi)
    o_ref[...] = r[:nb] + r[nb:] + _dot(c_hi, w_lo) + b_ref[...]


def _modulation(c, w_mod, b_mod):
    depth, d, n = w_mod.shape
    bsz = c.shape[0]
    tn = min(TN_MOD, n)
    return pl.pallas_call(
        _mod_kernel,
        grid=(depth, n // tn),
        in_specs=[
            pl.BlockSpec((bsz, d), lambda l, j: (0, 0)),
            pl.BlockSpec((None, d, tn), lambda l, j: (l, 0, j)),
            pl.BlockSpec((None, 1, tn), lambda l, j: (l, 0, j)),
        ],
        out_specs=pl.BlockSpec((None, bsz, tn), lambda l, j: (l, 0, j)),
        out_shape=jax.ShapeDtypeStruct((depth, bsz, n), F32),
        compiler_params=_cparams(("arbitrary", "arbitrary")),
        name="modulation",
    )(c, w_mod, b_mod.reshape(depth, 1, n))


def _mod_spec(l, j, per_b, d):
    return pl.BlockSpec((None, None, None, 1, d), lambda i: (l, i // per_b, j, 0, 0))


def _rms_modulate(x, nw, sc, sh):
    ms = jnp.mean(x * x, axis=-1, keepdims=True)
    return (x * lax.rsqrt(ms + EPS) * nw) * (1.0 + sc) + sh


def _inproj_kernel(x_ref, sh_ref, sc_ref, nw_ref, w_ref, wdt_ref, o_ref, odt_ref):
    hb = _rms_modulate(x_ref[...], nw_ref[...], sc_ref[...], sh_ref[...]).astype(BF16)
    n_main = o_ref.shape[1]
    for c0 in range(0, n_main, TN_PROJ):
        o_ref[:, c0:c0 + TN_PROJ] = _dot(hb, w_ref[:, c0:c0 + TN_PROJ]).astype(BF16)
    odt_ref[...] = _dot(hb, wdt_ref[...])


def _in_proj(l, x2d, mod, norm_w, w_in, w_dt, n_main, seq):
    t, d = x2d.shape
    tm = min(TM_PROJ, seq)
    per_b = seq // tm
    return pl.pallas_call(
        _inproj_kernel,
        grid=(t // tm,),
        in_specs=[
            pl.BlockSpec((tm, d), lambda i: (i, 0)),
            _mod_spec(l, 0, per_b, d),
            _mod_spec(l, 1, per_b, d),
            pl.BlockSpec((None, 1, d), lambda i: (l, 0, 0)),
            pl.BlockSpec((None, d, n_main), lambda i: (l, 0, 0), pipeline_mode=pl.Buffered(1)),
            pl.BlockSpec((None, d, LANES), lambda i: (l, 0, 0), pipeline_mode=pl.Buffered(1)),
        ],
        out_specs=[
            pl.BlockSpec((tm, n_main), lambda i: (i, 0)),
            pl.BlockSpec((tm, LANES), lambda i: (i, 0)),
        ],
        out_shape=[
            jax.ShapeDtypeStruct((t, n_main), BF16),
            jax.ShapeDtypeStruct((t, LANES), F32),
        ],
        compiler_params=_cparams(("arbitrary",)),
        name="in_proj",
    )(x2d, mod, mod, norm_w, w_in, w_dt)


def _inproj_combine_kernel(pos_hbm, ys_hbm, x1_ref, g2_ref, sh_ref, sc_ref, nw_ref, w_ref, wdt_ref,
                           x2_ref, o_ref, odt_ref, idx_smem, ybuf, sem, isem):
    i = pl.program_id(0)
    last = pl.num_programs(0) - 1
    tm, d = x1_ref.shape
    rows_per_tile = tm // LANES

    def gather(tile):
        _gather_rows(pos_hbm, tile * rows_per_tile, 0, ys_hbm, idx_smem, isem, ybuf, sem, _token_rows(d))

    @pl.when(i == 0)
    def _():
        gather(0)

    _wait_rows(ys_hbm, ybuf, sem)
    x2 = x1_ref[...] + g2_ref[...] * _get_tokens(ybuf, tm, d)[0]
    x2_ref[...] = x2
    gather(jnp.minimum(i + 1, last))
    hb = _rms_modulate(x2, nw_ref[...], sc_ref[...], sh_ref[...]).astype(BF16)
    n_main = o_ref.shape[1]
    for c0 in range(0, n_main, TN_PROJ):
        o_ref[:, c0:c0 + TN_PROJ] = _dot(hb, w_ref[:, c0:c0 + TN_PROJ]).astype(BF16)
    odt_ref[...] = _dot(hb, wdt_ref[...])

    @pl.when(i == last)
    def _():
        _wait_rows(ys_hbm, ybuf, sem)


def _in_proj_combine(l, pos_rows, ys, x1, mod, norm_w, w_in, w_dt, n_main, seq):
    t, d = x1.shape
    tm = min(TM_PROJ, seq)
    per_b = seq // tm
    hbm = pl.BlockSpec(memory_space=pl.ANY)
    return pl.pallas_call(
        _inproj_combine_kernel,
        grid=(t // tm,),
        in_specs=[
            hbm, hbm,
            pl.BlockSpec((tm, d), lambda i: (i, 0)),
            _mod_spec(l - 1, 5, per_b, d),
            _mod_spec(l, 0, per_b, d),
            _mod_spec(l, 1, per_b, d),
            pl.BlockSpec((None, 1, d), lambda i: (l, 0, 0)),
            pl.BlockSpec((None, d, n_main), lambda i: (l, 0, 0), pipeline_mode=pl.Buffered(1)),
            pl.BlockSpec((None, d, LANES), lambda i: (l, 0, 0), pipeline_mode=pl.Buffered(1)),
        ],
        out_specs=[
            pl.BlockSpec((tm, d), lambda i: (i, 0)),
            pl.BlockSpec((tm, n_main), lambda i: (i, 0)),
            pl.BlockSpec((tm, LANES), lambda i: (i, 0)),
        ],
        out_shape=[
            jax.ShapeDtypeStruct((t, d), F32),
            jax.ShapeDtypeStruct((t, n_main), BF16),
            jax.ShapeDtypeStruct((t, LANES), F32),
        ],
        scratch_shapes=[
            pltpu.SMEM((tm,), jnp.int32),
            pltpu.VMEM((tm * _token_rows(d), LANES), F32),
            pltpu.SemaphoreType.DMA(()),
            pltpu.SemaphoreType.DMA(()),
        ],
        compiler_params=_cparams(("arbitrary",)),
        name="in_proj_combine",
    )(pos_rows, ys, x1, mod, mod, mod, norm_w, w_in, w_dt)


def _conv_kernel(a_ref, g_ref, w_ref, b_ref, lnw_ref, lnb_ref, o_ref, ubuf):
    tl, ch = o_ref.shape
    taps = w_ref.shape[0]
    n = CONV_HALO + tl
    i = pl.program_id(1)

    @pl.when(i == 0)
    def _():
        ubuf[0, 0:CONV_HALO, :] = jnp.zeros((CONV_HALO, ch), F32)

    @pl.when(i > 0)
    def _():
        ubuf[0, 0:CONV_HALO, :] = ubuf[0, tl:tl + CONV_HALO, :]

    ubuf[0, CONV_HALO:n, :] = a_ref[...].astype(F32) * _sigmoid(g_ref[...].astype(F32))
    for s in range(1, SUBLANES):
        ubuf[s, 0:n - s, :] = ubuf[0, s:n, :]

    rows = min(CONV_ROWS, tl)
    for cb in range(ch // CONV_GROUP):
        cs = slice(cb * CONV_GROUP, (cb + 1) * CONV_GROUP)
        bias = b_ref[:, cs]
        lnw = lnw_ref[:, cs]
        lnb = lnb_ref[:, cs]
        for rs in range(tl // rows):
            r0 = CONV_HALO - (taps - 1) + rs * rows
            acc = jnp.zeros((rows, CONV_GROUP), F32)
            for k in range(taps):
                phase = (r0 + k) % SUBLANES
                a0 = r0 + k - phase
                acc = acc + w_ref[k:k + 1, cs] * ubuf[phase, a0:a0 + rows, cs]
            acc = acc + bias
            mu = jnp.mean(acc, axis=-1, keepdims=True)
            dev = acc - mu
            var = jnp.mean(dev * dev, axis=-1, keepdims=True)
            yn = dev * lax.rsqrt(var + EPS) * lnw + lnb
            o_ref[rs * rows:(rs + 1) * rows, cs] = (yn * _sigmoid(yn)).astype(o_ref.dtype)


def _conformer_conv(l, proj, dw_w, dw_b, ln_w, ln_b, bsz, seq):
    _, taps, ch = dw_w.shape
    tl = min(TL_CONV, seq)
    nl = seq // tl
    row = lambda b, i: b * nl + i
    vec = pl.BlockSpec((None, 1, ch), lambda b, i: (l, 0, 0))
    return pl.pallas_call(
        _conv_kernel,
        grid=(bsz, nl),
        in_specs=[
            pl.BlockSpec((tl, ch), lambda b, i: (row(b, i), 0)),
            pl.BlockSpec((tl, ch), lambda b, i: (row(b, i), 1)),
            pl.BlockSpec((None, taps, ch), lambda b, i: (l, 0, 0)),
            vec, vec, vec,
        ],
        out_specs=pl.BlockSpec((tl, ch), lambda b, i: (row(b, i), 0)),
        out_shape=jax.ShapeDtypeStruct((bsz * seq, ch), BF16),
        scratch_shapes=[pltpu.VMEM((SUBLANES, CONV_HALO + tl, ch), F32)],
        compiler_params=_cparams(("arbitrary", "arbitrary")),
        name="conformer_conv",
    )(proj, proj, dw_w, dw_b, ln_w, ln_b)


def _ssd_kernel(z_ref, xbc_ref, dt_ref, cw_ref, cb_ref, alog_ref, dtb_ref, dskip_ref, nw_ref,
                expand_ref, o_ref, xbuf, state):
    q, d_ssd = o_ref.shape
    gw = d_ssd // SSD_GROUPS
    gn = SSD_GROUPS * SSD_STATE
    taps = cw_ref.shape[0]
    c = pl.program_id(1)

    @pl.when(c == 0)
    def _():
        xbuf[0:SSD_HALO, :] = jnp.zeros((SSD_HALO, xbuf.shape[1]), F32)
        state[...] = jnp.zeros(state.shape, F32)

    @pl.when(c > 0)
    def _():
        xbuf[0:SSD_HALO, :] = xbuf[q:q + SSD_HALO, :]

    xbuf[SSD_HALO:SSD_HALO + q, :] = xbc_ref[...].astype(F32)
    acc = jnp.zeros((q, xbuf.shape[1]), F32) + cb_ref[...]
    for k in range(taps):
        r0 = SSD_HALO - (taps - 1) + k
        acc = acc + cw_ref[k:k + 1, :] * xbuf[r0:r0 + q, :]
    xc = acc * _sigmoid(acc)
    xs = xc[:, :d_ssd]
    bm = xc[:, d_ssd:d_ssd + gn]
    cm = xc[:, d_ssd + gn:]

    dt = jax.nn.softplus(dt_ref[...] + dtb_ref[...])
    a = -jnp.exp(alog_ref[...])
    da = dt * a
    rows = lax.broadcasted_iota(jnp.int32, (q, q), 0)
    cols = lax.broadcasted_iota(jnp.int32, (q, q), 1)
    tril = rows >= cols
    tri = jnp.where(tril, 1.0, 0.0).astype(BF16)
    d1, d2, d3 = _split3(da)
    cs = _dot(tri, d1) + _dot(tri, d2) + _dot(tri, d3)
    cs_t = cs.T
    cs_last = cs[q - 1:q, :]

    expand = expand_ref[...]

    def per_channel(v):
        hi, lo = _split2(v)
        return _dot(hi, expand) + _dot(lo, expand)

    dt_e = per_channel(dt)
    ecs_e = per_channel(jnp.exp(cs))
    wds_e = per_channel(dt * jnp.exp(cs_last - cs))
    xd = (xs * dt_e).astype(BF16)
    xds = (xs * wds_e).astype(BF16)
    chunk_decay = ecs_e[q - 1:q, :]

    lane = lax.broadcasted_iota(jnp.int32, (q, LANES), 1)
    heads_per_group = gw // SSD_HEAD_DIM
    y_parts = []
    for g in range(SSD_GROUPS):
        bg = bm[:, g * SSD_STATE:(g + 1) * SSD_STATE]
        cg = cm[:, g * SSD_STATE:(g + 1) * SSD_STATE].astype(BF16)
        cb_mat = _dot_nt(cg, bg.astype(BF16))
        st = state[g]
        y_off = _dot(cg, st.astype(BF16)) * ecs_e[:, g * gw:(g + 1) * gw]
        diag_parts = []
        for pr in range(heads_per_group // 2):
            lo_col = g * gw + pr * LANES
            xd_blk = xd[:, lo_col:lo_col + LANES]
            yp = jnp.zeros((q, LANES), F32)
            for half in range(2):
                h = g * heads_per_group + 2 * pr + half
                diff = cs[:, h:h + 1] - cs_t[h:h + 1, :]
                decay = jnp.exp(jnp.where(tril, diff, -jnp.inf))
                m = (cb_mat * decay).astype(BF16)
                keep = (lane >= SSD_HEAD_DIM) if half else (lane < SSD_HEAD_DIM)
                yp = yp + _dot(m, jnp.where(keep, xd_blk, jnp.zeros_like(xd_blk)))
            diag_parts.append(yp)
        y_parts.append(jnp.concatenate(diag_parts, axis=1) + y_off)
        new = _dot(bg.T.astype(BF16), xds[:, g * gw:(g + 1) * gw])
        state[g] = st * chunk_decay[:, g * gw:(g + 1) * gw] + new

    y = jnp.concatenate(y_parts, axis=1) + dskip_ref[...] * xs
    z = z_ref[...].astype(F32)
    y = y * (z * _sigmoid(z))
    outs = []
    for g in range(SSD_GROUPS):
        yg = y[:, g * gw:(g + 1) * gw]
        outs.append(yg * lax.rsqrt(jnp.mean(yg * yg, axis=-1, keepdims=True) + EPS))
    o_ref[...] = (jnp.concatenate(outs, axis=1) * nw_ref[...]).astype(o_ref.dtype)


def _ssd(l, proj, dt_raw, conv_w, conv_b, a_log, dt_bias, dskip_e, norm_w, expand, bsz, seq, z_col0):
    _, taps, xbc_w = conv_w.shape
    d_ssd = norm_w.shape[2]
    q = SSD_CHUNK
    nc = seq // q
    gw = d_ssd // SSD_GROUPS
    row = lambda b, c: b * nc + c
    vec = lambda n: pl.BlockSpec((None, 1, n), lambda b, c: (l, 0, 0))
    return pl.pallas_call(
        _ssd_kernel,
        grid=(bsz, nc),
        in_specs=[
            pl.BlockSpec((q, d_ssd), lambda b, c: (row(b, c), z_col0 // d_ssd)),
            pl.BlockSpec((q, xbc_w), lambda b, c: (row(b, c), (z_col0 + d_ssd) // xbc_w)),
            pl.BlockSpec((q, LANES), lambda b, c: (row(b, c), 0)),
            pl.BlockSpec((None, taps, xbc_w), lambda b, c: (l, 0, 0)),
            vec(xbc_w), vec(LANES), vec(LANES), vec(d_ssd), vec(d_ssd),
            pl.BlockSpec((LANES, d_ssd), lambda b, c: (0, 0)),
        ],
        out_specs=pl.BlockSpec((q, d_ssd), lambda b, c: (row(b, c), 0)),
        out_shape=jax.ShapeDtypeStruct((bsz * seq, d_ssd), BF16),
        scratch_shapes=[
            pltpu.VMEM((SSD_HALO + q, xbc_w), F32),
            pltpu.VMEM((SSD_GROUPS, SSD_STATE, gw), F32),
        ],
        compiler_params=_cparams(("arbitrary", "arbitrary")),
        name="ssd",
    )(proj, proj, dt_raw, conv_w, conv_b, a_log, dt_bias, dskip_e, norm_w, expand)


def _outproj_kernel(yc_ref, ys_ref, x_ref, g1_ref, sh2_ref, sc2_ref, nw_ref, w_ref, wr_hi_ref,
                    wr_lo_ref, rb_ref, x1_ref, h2_ref, qr_ref, cnt_ref, base):
    tm, d = x_ref.shape
    n_exp = wr_hi_ref.shape[0]
    dc = yc_ref.shape[1]
    i = pl.program_id(0)

    @pl.when(i == 0)
    def _():
        base[...] = jnp.zeros(base.shape, F32)

    mix = _dot(yc_ref[...], w_ref[0:dc, :]) + _dot(ys_ref[...], w_ref[dc:, :])
    x1 = x_ref[...] + g1_ref[...] * mix
    x1_ref[...] = x1
    h2 = _rms_modulate(x1, nw_ref[...], sc2_ref[...], sh2_ref[...])

    h_hi, h_lo = _split2(h2)
    wr_hi = wr_hi_ref[...]
    logits = _dot_nt(wr_hi, h_hi) + _dot_nt(wr_lo_ref[...], h_hi) + _dot_nt(wr_hi, h_lo)
    s = _sigmoid(logits)
    s_sel = s + rb_ref[...]
    eiota = lax.broadcasted_iota(jnp.int32, s.shape, 0)
    group_of = lax.shift_right_logical(eiota, EXPERTS_PER_GROUP.bit_length() - 1)
    in_group = jnp.bitwise_and(eiota, EXPERTS_PER_GROUP - 1)
    pair_max = jnp.full(s.shape, -jnp.inf, F32)
    for j in range(1, EXPERTS_PER_GROUP):
        partner = pltpu.roll(s_sel, n_exp - j, 0)
        pair_max = jnp.maximum(pair_max, jnp.where(in_group < EXPERTS_PER_GROUP - j,
                                                   s_sel + partner, -jnp.inf))
    gmax = jnp.max(pair_max, axis=0, keepdims=True)
    best = jnp.min(jnp.where(pair_max == gmax, group_of, N_EXPERT_GROUPS), axis=0, keepdims=True)
    masked = jnp.where(group_of == best, s_sel, -jnp.inf)
    m1 = jnp.max(masked, axis=0, keepdims=True)
    i1 = jnp.min(jnp.where(masked == m1, eiota, n_exp), axis=0, keepdims=True)
    masked = jnp.where(eiota == i1, -jnp.inf, masked)
    m2 = jnp.max(masked, axis=0, keepdims=True)
    i2 = jnp.min(jnp.where(masked == m2, eiota, n_exp), axis=0, keepdims=True)
    e_lo = jnp.minimum(i1, i2)
    e_hi = jnp.maximum(i1, i2)
    a_lo = jnp.sum(jnp.where(eiota == e_lo, s, 0.0), axis=0, keepdims=True)
    a_hi = jnp.sum(jnp.where(eiota == e_hi, s, 0.0), axis=0, keepdims=True)
    tot = a_lo + a_hi
    liota = lax.broadcasted_iota(jnp.int32, (LANES, tm), 0)
    gates = jnp.where(liota == 0, a_lo / tot, jnp.where(liota == 1, a_hi / tot, 0.0))
    _put_tokens(h2_ref, h2, gates.T)
    in_lo = jnp.bitwise_and(e_lo, EXPERTS_PER_GROUP - 1)
    in_hi = jnp.bitwise_and(e_hi, EXPERTS_PER_GROUP - 1)
    bucket = (lax.shift_right_logical(e_lo, EXPERTS_PER_GROUP.bit_length() - 1) * PAIRS_PER_GROUP
              + _pair_index(in_lo, in_hi))
    qr_ref[0:1, :] = bucket

    oh = jnp.where(lax.broadcasted_iota(jnp.int32, (base.shape[0], tm), 0) == bucket, 1.0, 0.0)
    r_i = lax.broadcasted_iota(jnp.int32, (tm, tm), 0)
    c_i = lax.broadcasted_iota(jnp.int32, (tm, tm), 1)
    upper = jnp.where(r_i <= c_i, 1.0, 0.0).astype(BF16)
    incl = _dot(oh.astype(BF16), upper)
    rank_all = base[...] + incl - oh
    qr_ref[1:2, :] = jnp.sum(oh * rank_all, axis=0, keepdims=True).astype(jnp.int32)
    new_base = base[...] + jnp.sum(oh, axis=1, keepdims=True)
    base[...] = new_base
    cnt_ref[...] = jnp.broadcast_to(new_base, cnt_ref.shape).astype(jnp.int32)


def _out_proj(l, y_conv, y_ssd, x2d, mod, norm_w, w_out, wr_hi, wr_lo, router_bias, seq):
    t, d = x2d.shape
    dc = y_conv.shape[1]
    ds = y_ssd.shape[1]
    n_exp = wr_hi.shape[0]
    tm = min(TM_PROJ, seq)
    per_b = seq // tm
    const = lambda shape: pl.BlockSpec(shape, lambda i: (0, 0))
    return pl.pallas_call(
        _outproj_kernel,
        grid=(t // tm,),
        in_specs=[
            pl.BlockSpec((tm, dc), lambda i: (i, 0)),
            pl.BlockSpec((tm, ds), lambda i: (i, 0)),
            pl.BlockSpec((tm, d), lambda i: (i, 0)),
            _mod_spec(l, 2, per_b, d), _mod_spec(l, 3, per_b, d), _mod_spec(l, 4, per_b, d),
            pl.BlockSpec((None, 1, d), lambda i: (l, 0, 0)),
            pl.BlockSpec((None, dc + ds, d), lambda i: (l, 0, 0), pipeline_mode=pl.Buffered(1)),
            const((n_exp, d)), const((n_exp, d)), const((n_exp, 1)),
        ],
        out_specs=[
            pl.BlockSpec((tm, d), lambda i: (i, 0)),
            pl.BlockSpec((tm * _token_rows(d), LANES), lambda i: (i, 0)),
            pl.BlockSpec((2, tm), lambda i: (0, i)),
            const((BUCKET_ROWS, LANES)),
        ],
        out_shape=[
            jax.ShapeDtypeStruct((t, d), F32),
            jax.ShapeDtypeStruct((t * _token_rows(d), LANES), F32),
            jax.ShapeDtypeStruct((2, t), jnp.int32),
            jax.ShapeDtypeStruct((BUCKET_ROWS, LANES), jnp.int32),
        ],
        scratch_shapes=[pltpu.VMEM((BUCKET_ROWS, 1), F32)],
        compiler_params=_cparams(("arbitrary",)),
        name="out_proj_router",
    )(y_conv, y_ssd, x2d, mod, mod, mod, norm_w, w_out, wr_hi, wr_lo,
      router_bias.reshape(n_exp, 1).astype(F32))


def _index_copies(idx_hbm, row0, idx_smem, isem):
    return [pltpu.make_async_copy(idx_hbm.at[row0 + i], idx_smem.at[pl.ds(i * LANES, LANES)], isem)
            for i in range(idx_smem.shape[0] // LANES)]


def _token_rows(d):
    return d // LANES + 1


def _put_tokens(ref, values, extra):
    tm, d = values.shape
    span = _token_rows(d)
    for cc in range(span - 1):
        ref[pl.ds(cc, tm, stride=span), :] = values[:, cc * LANES:(cc + 1) * LANES]
    ref[pl.ds(span - 1, tm, stride=span), :] = extra


def _get_tokens(ref, tm, d):
    span = _token_rows(d)
    chunks = [ref[pl.ds(cc, tm, stride=span), :] for cc in range(span)]
    return jnp.concatenate(chunks[:-1], axis=1), chunks[-1]


def _start_rows(idx_smem, delta, src_hbm, buf, sem, span=1):
    for r in range(buf.shape[0] // span):
        pltpu.make_async_copy(src_hbm.at[pl.ds(idx_smem[delta + r] * span, span)],
                              buf.at[pl.ds(r * span, span)], sem).start()


def _gather_rows(idx_hbm, row0, delta, src_hbm, idx_smem, isem, buf, sem, span=1):
    copies = _index_copies(idx_hbm, row0, idx_smem, isem)
    for cp in copies:
        cp.start()
    for cp in copies:
        cp.wait()
    _start_rows(idx_smem, delta, src_hbm, buf, sem, span)


def _wait_rows(src_hbm, buf, sem):
    pltpu.make_async_copy(src_hbm.at[pl.ds(0, buf.shape[0])], buf, sem).wait()


def _pair_index(lo, hi):
    idx = 0
    for p, (a, b) in enumerate(PAIRS):
        idx = jnp.where((lo == a) & (hi == b), p, idx)
    return idx


def _pair_members(p):
    lo = hi = 0
    for k, (a, b) in enumerate(PAIRS):
        lo = jnp.where(p == k, a, lo)
        hi = jnp.where(p == k, b, hi)
    return lo, hi


WEIGHT_EVENTS = {0: (2, 3), 1: (3, 4), 3: (0, 5), 5: (1, 6)}


def _expert_kernel(bq_ref, first_ref, src_ref, nused_ref, tok_hbm, h_hbm, wg_hbm, wu_hbm, wd_hbm, o_ref,
                   idx0, idx1, idx2, xbuf0, xbuf1, xbuf2, sem, isem, wsem, sg, su, sd, wg_bf, wu_bf, wd_bf,
                   *, layer):
    s = pl.program_id(0)
    b = s - 1
    n_used = nused_ref[0]
    n_exp = wg_hbm.shape[1]
    d = wg_hbm.shape[2]
    span = _token_rows(d)
    tm = o_ref.shape[0] // span
    idx = (idx0, idx1, idx2)
    xbuf = (xbuf0, xbuf1, xbuf2)
    nbuf = len(xbuf)
    stage = ((wg_hbm, sg, wg_bf), (wu_hbm, su, wu_bf), (wd_hbm, sd, wd_bf))
    b_tab = jnp.clip(b, 0, bq_ref.shape[0] - 1)
    group = bq_ref[b_tab] // PAIRS_PER_GROUP
    pair = lax.rem(bq_ref[b_tab], PAIRS_PER_GROUP)

    def index_copies(blk, p):
        src = src_ref[jnp.minimum(blk, n_used - 1)]
        return _index_copies(tok_hbm, lax.shift_right_logical(src, LANE_SHIFT), idx[p], isem)

    def start_rows(blk, p):
        src = src_ref[jnp.minimum(blk, n_used - 1)]
        _start_rows(idx[p], jnp.bitwise_and(src, LANES - 1), h_hbm, xbuf[p], sem.at[p], span)

    def weights_start(e):
        for j, (w_hbm, st, _) in enumerate(stage):
            pltpu.make_async_copy(w_hbm.at[layer, e], st, wsem.at[j]).start()

    def weights_finish(slot):
        for j, (w_hbm, st, w_bf) in enumerate(stage):
            pltpu.make_async_copy(w_hbm.at[layer, 0], st, wsem.at[j]).wait()
            w_bf[slot] = st[...].astype(BF16)

    def compute(p):
        lo, hi = _pair_members(pair)
        x, gates = _get_tokens(xbuf[p], tm, d)
        x = x.astype(BF16)
        out = None
        for slot, col in ((lo, 0), (hi, 1)):
            gate = _dot(x, wg_bf[slot])
            up = _dot(x, wu_bf[slot])
            hb = (gate * _sigmoid(gate) * up).astype(BF16)
            part = _dot(hb, wd_bf[slot]) * gates[:, col:col + 1]
            out = part if out is None else out + part
        _put_tokens(o_ref, out, jnp.zeros((tm, LANES), F32))

    @pl.when(s == 0)
    def _():
        for k in range(2):
            for cp in index_copies(k, k):
                cp.start()
                cp.wait()
            start_rows(k, k)
        for cp in index_copies(2, 2):
            cp.start()
        for e in range(2):
            weights_start(e)
            weights_finish(e)
        weights_start(2)

    is_first = (b >= 0) & (b < n_used) & (first_ref[b_tab] == 1)
    for ev_pair, (slot, nxt) in WEIGHT_EVENTS.items():
        e_next = group * EXPERTS_PER_GROUP + nxt

        @pl.when(is_first & (pair == ev_pair) & (group * EXPERTS_PER_GROUP + nxt - 1 < n_exp))
        def _():
            weights_finish(slot)

            @pl.when(e_next < n_exp)
            def _():
                weights_start(e_next)

    for p in range(nbuf):
        @pl.when((b >= 0) & (b < n_used) & (lax.rem(b, nbuf) == p))
        def _():
            for cp in index_copies(s + 1, (p + 2) % nbuf):
                cp.wait()
            start_rows(s + 1, (p + 2) % nbuf)
            for cp in index_copies(s + 2, p):
                cp.start()

        @pl.when((b >= 0) & (b < n_used) & (lax.rem(b, nbuf) == p))
        def _():
            _wait_rows(h_hbm, xbuf[p], sem.at[p])
            compute(p)

        @pl.when((b >= n_used) & (b < n_used + 2) & (lax.rem(b, nbuf) == p))
        def _():
            _wait_rows(h_hbm, xbuf[p], sem.at[p])

            @pl.when(b == n_used)
            def _():
                for cp in index_copies(s + 1, (p + 2) % nbuf):
                    cp.wait()

    @pl.when(b >= n_used)
    def _():
        o_ref[...] = jnp.zeros(o_ref.shape, o_ref.dtype)


def _experts(l, h2x, block_bucket, block_first, block_src, n_used, tok_rows, w_gate, w_up, w_down, tm):
    nb = block_bucket.shape[0]
    _, n_exp, d, de = w_gate.shape
    span = _token_rows(d)
    assert n_exp == N_EXPERT_GROUPS * EXPERTS_PER_GROUP and h2x.shape[1] == LANES and span % 2 == 1
    hbm = pl.BlockSpec(memory_space=pl.ANY)
    grid_spec = pltpu.PrefetchScalarGridSpec(
        num_scalar_prefetch=4,
        grid=(nb + 2,),
        in_specs=[hbm, hbm, hbm, hbm, hbm],
        out_specs=pl.BlockSpec((tm * span, LANES), lambda s, *_: (jnp.clip(s - 1, 0, nb - 1), 0)),
        scratch_shapes=[
            pltpu.SMEM((tm + LANES,), jnp.int32),
            pltpu.SMEM((tm + LANES,), jnp.int32),
            pltpu.SMEM((tm + LANES,), jnp.int32),
            pltpu.VMEM((tm * span, LANES), F32),
            pltpu.VMEM((tm * span, LANES), F32),
            pltpu.VMEM((tm * span, LANES), F32),
            pltpu.SemaphoreType.DMA((3,)),
            pltpu.SemaphoreType.DMA(()),
            pltpu.SemaphoreType.DMA((3,)),
            pltpu.VMEM((d, de), F32),
            pltpu.VMEM((d, de), F32),
            pltpu.VMEM((de, d), F32),
            pltpu.VMEM((EXPERTS_PER_GROUP, d, de), BF16),
            pltpu.VMEM((EXPERTS_PER_GROUP, d, de), BF16),
            pltpu.VMEM((EXPERTS_PER_GROUP, de, d), BF16),
        ],
    )
    return pl.pallas_call(
        functools.partial(_expert_kernel, layer=l),
        grid_spec=grid_spec,
        out_shape=jax.ShapeDtypeStruct((nb * tm * span, LANES), F32),
        compiler_params=_cparams(("arbitrary",)),
        name="experts",
    )(block_bucket, block_first, block_src, n_used, tok_rows, h2x, w_gate, w_up, w_down)


def _combine_kernel(pos_hbm, ys_hbm, x1_ref, g2_ref, fnw_ref, o_ref,
                    idx0, idx1, ybuf0, ybuf1, sem, isem):
    s = pl.program_id(0)
    nt = pl.num_programs(0) - 1
    b = s - 1
    tm, d = x1_ref.shape
    span = _token_rows(d)
    idx = (idx0, idx1)
    ybuf = (ybuf0, ybuf1)
    rows_per_tile = tm // LANES

    def index_copies(tile, p):
        return _index_copies(pos_hbm, jnp.minimum(tile, nt - 1) * rows_per_tile, idx[p], isem)

    def finish(p):
        x2 = x1_ref[...] + g2_ref[...] * _get_tokens(ybuf[p], tm, d)[0]
        ms = jnp.mean(x2 * x2, axis=-1, keepdims=True)
        o_ref[...] = x2 * lax.rsqrt(ms + EPS) * fnw_ref[...]

    @pl.when(s == 0)
    def _():
        for cp in index_copies(0, 0):
            cp.start()
            cp.wait()
        _start_rows(idx[0], 0, ys_hbm, ybuf[0], sem.at[0], span)
        for cp in index_copies(1, 1):
            cp.start()

    for p in range(2):
        @pl.when((b >= 0) & (s < nt) & (lax.rem(b, 2) == p))
        def _():
            _wait_rows(ys_hbm, ybuf[p], sem.at[p])
            for cp in index_copies(s, 1 - p):
                cp.wait()
            _start_rows(idx[1 - p], 0, ys_hbm, ybuf[1 - p], sem.at[1 - p], span)
            for cp in index_copies(s + 1, p):
                cp.start()
            finish(p)

        @pl.when((s == nt) & (lax.rem(b, 2) == p))
        def _():
            _wait_rows(ys_hbm, ybuf[p], sem.at[p])
            for cp in index_copies(s, 1 - p):
                cp.wait()
            finish(p)


def _combine(l, pos_rows, ys, x1, mod, final_norm_w, seq, tm):
    t, d = x1.shape
    nt = t // tm
    per_b = seq // tm
    tile = lambda s: jnp.maximum(s - 1, 0)
    return pl.pallas_call(
        _combine_kernel,
        grid=(nt + 1,),
        in_specs=[
            pl.BlockSpec(memory_space=pl.ANY),
            pl.BlockSpec(memory_space=pl.ANY),
            pl.BlockSpec((tm, d), lambda s: (tile(s), 0)),
            pl.BlockSpec((None, None, None, 1, d), lambda s: (l, tile(s) // per_b, 5, 0, 0)),
            pl.BlockSpec((1, d), lambda s: (0, 0)),
        ],
        out_specs=pl.BlockSpec((tm, d), lambda s: (tile(s), 0)),
        out_shape=jax.ShapeDtypeStruct((t, d), F32),
        scratch_shapes=[
            pltpu.SMEM((tm,), jnp.int32),
            pltpu.SMEM((tm,), jnp.int32),
            pltpu.VMEM((tm * _token_rows(d), LANES), F32),
            pltpu.VMEM((tm * _token_rows(d), LANES), F32),
            pltpu.SemaphoreType.DMA((2,)),
            pltpu.SemaphoreType.DMA(()),
        ],
        compiler_params=_cparams(("arbitrary",)),
        name="combine",
    )(pos_rows, ys, x1, mod, final_norm_w.reshape(1, d).astype(F32))


def _routing_tables(bucket, rank, counts, tm_e):
    nq = counts.shape[0]
    t = bucket.shape[0]
    nb = t // tm_e + nq
    buckets = jnp.arange(nq, dtype=jnp.int32)
    padded = jnp.maximum((counts + tm_e - 1) // tm_e, 1) * tm_e
    pends = jnp.cumsum(padded)
    pstarts = pends - padded
    starts = jnp.cumsum(counts) - counts
    pos = jnp.sum(jnp.where(bucket[None, :] == buckets[:, None], pstarts[:, None], 0), axis=0) + rank
    n_used = (pends[-1] // tm_e).astype(jnp.int32).reshape(1)
    first_slot = jnp.arange(nb, dtype=jnp.int32) * tm_e
    block_bucket = jnp.minimum(jnp.sum(pends[None, :] <= first_slot[:, None], axis=1), nq - 1).astype(jnp.int32)
    onehot = block_bucket[:, None] == buckets[None, :]
    block_first = (jnp.sum(jnp.where(onehot, pstarts[None, :], 0), axis=1) == first_slot).astype(jnp.int32)
    shift = jnp.sum(jnp.where(onehot, (pstarts - starts)[None, :], 0), axis=1)
    block_src = jnp.clip(first_slot - shift, 0, t - 1).astype(jnp.int32)
    keys = jnp.sort(bucket * t + jnp.arange(t, dtype=jnp.int32))
    tok_sorted = keys % t
    n_rows = t // LANES + tm_e // LANES + 1
    tok_rows = jnp.zeros((n_rows * LANES,), jnp.int32).at[:t].set(tok_sorted).reshape(n_rows, LANES)
    return block_bucket, block_first, block_src, n_used, tok_rows, pos.reshape(t // LANES, LANES)


def kernel(x, c, w_mod, b_mod, norm1_w, w_in, conv_dw_w, conv_dw_b, conv_ln_w, conv_ln_b,
           ssd_conv_w, ssd_conv_b, a_log, dt_bias, d_skip, ssd_norm_w, w_out, norm2_w, w_router,
           router_bias, w_gate, w_up, w_down, final_norm_w):
    bsz, seq, d = x.shape
    depth = w_mod.shape[0]
    t = bsz * seq
    d_conv = conv_dw_w.shape[2]
    d_ssd = ssd_norm_w.shape[1]
    xbc_w = ssd_conv_w.shape[2]
    heads = a_log.shape[1]
    n_main = w_in.shape[2] - heads
    tm_e = min(TM_EXPERT, seq)
    tm_c = min(TM_COMBINE, seq)
    assert seq % SSD_CHUNK == 0 and n_main % TN_PROJ == 0 and d_conv % CONV_GROUP == 0
    assert 2 * d_conv % d_ssd == 0 and (2 * d_conv + d_ssd) % xbc_w == 0
    assert tm_e % LANES == 0 and tm_c % LANES == 0 and heads <= LANES

    row3 = lambda v: v.reshape(depth, 1, -1).astype(F32)
    lane_pad = lambda v: jnp.zeros((depth, 1, LANES), F32).at[:, 0, :heads].set(v.astype(F32))
    w_in_bf = w_in[:, :, :n_main].astype(BF16)
    w_dt = jnp.zeros((depth, d, LANES), BF16).at[:, :, :heads].set(w_in[:, :, n_main:].astype(BF16))
    w_out_bf = w_out.astype(BF16)
    wr_hi, wr_lo = _split2(w_router.T)
    head_of_ch = jnp.arange(d_ssd, dtype=jnp.int32) // SSD_HEAD_DIM
    expand = (jnp.arange(LANES, dtype=jnp.int32)[:, None] == head_of_ch[None, :]).astype(BF16)
    dskip_e = jnp.repeat(d_skip.astype(F32), SSD_HEAD_DIM, axis=1).reshape(depth, 1, d_ssd)

    mod = _modulation(c, w_mod, b_mod).reshape(depth, bsz, 6, 1, d)
    x2d = x.reshape(t, d)
    for l in range(depth):
        if l == 0:
            proj, dt_raw = _in_proj(l, x2d, mod, row3(norm1_w), w_in_bf, w_dt, n_main, seq)
        else:
            x2d, proj, dt_raw = _in_proj_combine(l, pos_rows, ys, x1, mod, row3(norm1_w), w_in_bf, w_dt,
                                                 n_main, seq)
        y_conv = _conformer_conv(l, proj, conv_dw_w, row3(conv_dw_b), row3(conv_ln_w), row3(conv_ln_b),
                                 bsz, seq)
        y_ssd = _ssd(l, proj, dt_raw, ssd_conv_w, row3(ssd_conv_b), lane_pad(a_log), lane_pad(dt_bias),
                     dskip_e, row3(ssd_norm_w), expand, bsz, seq, 2 * d_conv)
        x1, h2x, qr, cnt = _out_proj(l, y_conv, y_ssd, x2d, mod, row3(norm2_w), w_out_bf,
                                     wr_hi, wr_lo, router_bias, seq)
        block_bucket, block_first, block_src, n_used, tok_rows, pos_rows = _routing_tables(
            qr[0], qr[1], cnt[:N_BUCKETS, 0], tm_e)
        ys = _experts(l, h2x, block_bucket, block_first, block_src, n_used, tok_rows, w_gate, w_up, w_down, tm_e)
    out = _combine(depth - 1, pos_rows, ys, x1, mod, final_norm_w, seq, tm_c)
    return out.reshape(bsz, seq, d)
```
